```python
import math
import jax
import jax.numpy as jnp
from jax import lax
import numpy as np

D_MODEL = 2048
BATCH = 2
SEQ = 8192
DEPTH = 4

CHUNK = 64
D_FF = 5632
CONV_W = 4
NORM_EPS = 1e-6
D_RNN = D_MODEL
LRU_BLOCKS = 16
LRU_BLOCK = D_RNN // LRU_BLOCKS
LRU_C = 8.0
DN_K_HEADS = 16
DN_V_HEADS = 32
DN_HEAD_K = 128
DN_HEAD_V = 128
DN_Q_DIM = DN_K_HEADS * DN_HEAD_K
DN_V_DIM = DN_V_HEADS * DN_HEAD_V
DN_QKV_DIM = 2 * DN_Q_DIM + DN_V_DIM
N_BRANCHES = 2
IN_SPLITS = (D_RNN, D_RNN, DN_Q_DIM, DN_Q_DIM, DN_V_DIM, DN_V_DIM, DN_V_HEADS, DN_V_HEADS, N_BRANCHES * D_MODEL)
IN_DIM = sum(IN_SPLITS)

kernel_name = "hybrid_rglru_gdn_macaron_trunk"


def rms_norm(x, g):
    xf = x.astype(jnp.float32)
    y = xf * lax.rsqrt(jnp.mean(xf * xf, axis=-1, keepdims=True) + NORM_EPS)
    return (y * g.astype(jnp.float32)).astype(x.dtype)


def swiglu_ffn(h, w_in, w_out):
    gate, up = jnp.split(h @ w_in, 2, axis=-1)
    return (jax.nn.silu(gate) * up) @ w_out


def causal_dwconv(x, w, b=None):
    K, C = w.shape
    y = lax.conv_general_dilated(
        x, w[:, None, :].astype(x.dtype), window_strides=(1,), padding=[(K - 1, 0)],
        dimension_numbers=("NWC", "WIO", "NWC"), feature_group_count=C)
    return y if b is None else y + b.astype(x.dtype)


def l2_normalize(t):
    tf = t.astype(jnp.float32)
    return tf * lax.rsqrt(jnp.sum(tf * tf, axis=-1, keepdims=True) + NORM_EPS)


def rg_lru(x, gate_a_w, gate_a_b, gate_x_w, gate_x_b, lam):
    B, T, _ = x.shape
    xf = x.astype(jnp.float32)
    xb = xf.reshape(B, T, LRU_BLOCKS, LRU_BLOCK)
    r = jax.nn.sigmoid(jnp.einsum("btgi,gij->btgj", xb, gate_a_w.astype(jnp.float32)).reshape(B, T, D_RNN) + gate_a_b.astype(jnp.float32))
    i = jax.nn.sigmoid(jnp.einsum("btgi,gij->btgj", xb, gate_x_w.astype(jnp.float32)).reshape(B, T, D_RNN) + gate_x_b.astype(jnp.float32))
    log_a = -LRU_C * r * jax.nn.softplus(-lam.astype(jnp.float32))
    a = jnp.exp(log_a)
    u = xf * i * jnp.sqrt(-jnp.expm1(2.0 * log_a))

    def combine(c1, c2):
        a1, b1 = c1
        a2, b2 = c2
        return a1 * a2, a2 * b1 + b2

    _, h = lax.associative_scan(combine, (a, u), axis=1)
    return h.astype(x.dtype)


def gated_delta_rule(q, k, v, g, beta):
    B, H, T, dk = q.shape
    dv = v.shape[-1]
    N = T // CHUNK
    q = q * (dk ** -0.5)
    q = q.reshape(B, H, N, CHUNK, dk)
    k = k.reshape(B, H, N, CHUNK, dk)
    v = v.reshape(B, H, N, CHUNK, dv)
    beta = beta.reshape(B, H, N, CHUNK)
    g = jnp.cumsum(g.reshape(B, H, N, CHUNK), axis=-1)
    causal = jnp.tril(jnp.ones((CHUNK, CHUNK), dtype=bool))
    strict = jnp.tril(jnp.ones((CHUNK, CHUNK), dtype=bool), -1)
    decay = jnp.exp(jnp.where(causal, g[..., :, None] - g[..., None, :], -jnp.inf))
    k_beta = k * beta[..., None]
    v_beta = v * beta[..., None]
    L = jnp.where(strict, jnp.einsum("bhncd,bhnsd->bhncs", k_beta, k) * decay, 0.0)
    tri = L + jnp.eye(CHUNK, dtype=L.dtype)
    rhs = jnp.concatenate([v_beta, k_beta * jnp.exp(g)[..., None]], axis=-1)
    sol = lax.linalg.triangular_solve(tri, rhs, left_side=True, lower=True, unit_diagonal=True)
    u_c, w_c = sol[..., :dv], sol[..., dv:]
    a_qk = jnp.einsum("bhncd,bhnsd->bhncs", q, k) * decay
    q_dec = q * jnp.exp(g)[..., None]
    k_dec = k * jnp.exp(g[..., -1:] - g)[..., None]
    g_last = jnp.exp(g[..., -1])
    xs = tuple(jnp.moveaxis(t, 2, 0) for t in (q_dec, k_dec, u_c, w_c, a_qk, g_last))

    def step(S, inp):
        qd, kd, uc, wc, aq, gl = inp
        v_new = uc - jnp.einsum("bhck,bhkv->bhcv", wc, S)
        o = jnp.einsum("bhck,bhkv->bhcv", qd, S) + jnp.einsum("bhcs,bhsv->bhcv", aq, v_new)
        S = S * gl[..., None, None] + jnp.einsum("bhck,bhcv->bhkv", kd, v_new)
        return S, o

    S0 = jnp.zeros((B, H, dk, dv), jnp.float32)
    _, o = lax.scan(step, S0, xs)
    return jnp.moveaxis(o, 0, 2).reshape(B, H, T, dv)


def hybrid_mixer(h, w_in, lru_conv_w, lru_conv_b, lru_gate_a_w, lru_gate_a_b, lru_gate_x_w, lru_gate_x_b,
                 lru_lambda, dn_conv_w, dn_A_log, dn_dt_bias, dn_norm, w_branch_lru, w_branch_dn, w_out):
    B, T, _ = h.shape
    offsets = []
    acc = 0
    for s in IN_SPLITS[:-1]:
        acc += s
        offsets.append(acc)
    xr, yr, q, k, v, z, b_raw, a_raw, gates = jnp.split(h @ w_in, offsets, axis=-1)

    xr = causal_dwconv(xr, lru_conv_w, lru_conv_b)
    y_lru = rg_lru(xr, lru_gate_a_w, lru_gate_a_b, lru_gate_x_w, lru_gate_x_b, lru_lambda) * jax.nn.gelu(yr)

    qkv = jax.nn.silu(causal_dwconv(jnp.concatenate([q, k, v], axis=-1), dn_conv_w))
    q, k, v = jnp.split(qkv, [DN_Q_DIM, 2 * DN_Q_DIM], axis=-1)
    rep = DN_V_HEADS // DN_K_HEADS
    q = jnp.repeat(l2_normalize(q.reshape(B, T, DN_K_HEADS, DN_HEAD_K)).transpose(0, 2, 1, 3), rep, axis=1)
    k = jnp.repeat(l2_normalize(k.reshape(B, T, DN_K_HEADS, DN_HEAD_K)).transpose(0, 2, 1, 3), rep, axis=1)
    v = v.reshape(B, T, DN_V_HEADS, DN_HEAD_V).transpose(0, 2, 1, 3).astype(jnp.float32)
    beta = jax.nn.sigmoid(b_raw.astype(jnp.float32)).transpose(0, 2, 1)
    g = (-jnp.exp(dn_A_log.astype(jnp.float32)) * jax.nn.softplus(a_raw.astype(jnp.float32) + dn_dt_bias.astype(jnp.float32))).transpose(0, 2, 1)
    o = gated_delta_rule(q, k, v, g, beta).transpose(0, 2, 1, 3)
    o = rms_norm(o, dn_norm) * jax.nn.silu(z.reshape(B, T, DN_V_HEADS, DN_HEAD_V).astype(jnp.float32))
    y_dn = o.reshape(B, T, DN_V_DIM).astype(h.dtype)

    g_lru, g_dn = jnp.split(jax.nn.sigmoid(gates), 2, axis=-1)
    merged = g_lru * (y_lru @ w_branch_lru) + g_dn * (y_dn @ w_branch_dn)
    return merged @ w_out


def setup_inputs(seed: int = 0) -> dict:
    key = jax.random.key(seed)
    ks = iter(jax.random.split(key, 40))
    f32 = jnp.float32
    L = DEPTH

    def nrm(shape, scale):
        return jax.random.normal(next(ks), shape, f32) * scale

    def gain(shape):
        return 1.0 + 0.05 * jax.random.normal(next(ks), shape, f32)

    x = nrm((BATCH, SEQ, D_MODEL), 1.0)
    ffn1_norm = gain((L, D_MODEL))
    ffn1_w_in = nrm((L, D_MODEL, 2 * D_FF), D_MODEL ** -0.5)
    ffn1_w_out = nrm((L, D_FF, D_MODEL), D_FF ** -0.5)
    mix_norm = gain((L, D_MODEL))
    w_in = nrm((L, D_MODEL, IN_DIM), D_MODEL ** -0.5)
    lru_conv_w = nrm((L, CONV_W, D_RNN), CONV_W ** -0.5)
    lru_conv_b = nrm((L, D_RNN), 0.02)
    lru_gate_a_w = nrm((L, LRU_BLOCKS, LRU_BLOCK, LRU_BLOCK), LRU_BLOCK ** -0.5)
    lru_gate_a_b = nrm((L, D_RNN), 0.02)
    lru_gate_x_w = nrm((L, LRU_BLOCKS, LRU_BLOCK, LRU_BLOCK), LRU_BLOCK ** -0.5)
    lru_gate_x_b = nrm((L, D_RNN), 0.02)
    a_c = jax.random.uniform(next(ks), (L, D_RNN), f32, 0.9, 0.999)
    s = a_c ** (1.0 / LRU_C)
    lru_lambda = jnp.log(s) - jnp.log1p(-s)
    dn_conv_w = nrm((L, CONV_W, DN_QKV_DIM), CONV_W ** -0.5)
    dn_A_log = jnp.log(jax.random.uniform(next(ks), (L, DN_V_HEADS), f32, 1.0, 16.0))
    dt = jnp.exp(jax.random.uniform(next(ks), (L, DN_V_HEADS), f32, math.log(1e-3), math.log(1e-1)))
    dn_dt_bias = dt + jnp.log(-jnp.expm1(-dt))
    dn_norm = gain((L, DN_HEAD_V))
    w_branch_lru = nrm((L, D_RNN, D_MODEL), D_RNN ** -0.5)
    w_branch_dn = nrm((L, DN_V_DIM, D_MODEL), DN_V_DIM ** -0.5)
    w_out = nrm((L, D_MODEL, D_MODEL), D_MODEL ** -0.5)
    ffn2_norm = gain((L, D_MODEL))
    ffn2_w_in = nrm((L, D_MODEL, 2 * D_FF), D_MODEL ** -0.5)
    ffn2_w_out = nrm((L, D_FF, D_MODEL), D_FF ** -0.5)
    final_norm = gain((D_MODEL,))
    return {"x": x, "ffn1_norm": ffn1_norm, "ffn1_w_in": ffn1_w_in, "ffn1_w_out": ffn1_w_out,
            "mix_norm": mix_norm, "w_in": w_in, "lru_conv_w": lru_conv_w, "lru_conv_b": lru_conv_b,
            "lru_gate_a_w": lru_gate_a_w, "lru_gate_a_b": lru_gate_a_b, "lru_gate_x_w": lru_gate_x_w,
            "lru_gate_x_b": lru_gate_x_b, "lru_lambda": lru_lambda, "dn_conv_w": dn_conv_w,
            "dn_A_log": dn_A_log, "dn_dt_bias": dn_dt_bias, "dn_norm": dn_norm,
            "w_branch_lru": w_branch_lru, "w_branch_dn": w_branch_dn, "w_out": w_out,
            "ffn2_norm": ffn2_norm, "ffn2_w_in": ffn2_w_in, "ffn2_w_out": ffn2_w_out,
            "final_norm": final_norm}


def reference(x, ffn1_norm, ffn1_w_in, ffn1_w_out, mix_norm, w_in, lru_conv_w, lru_conv_b,
              lru_gate_a_w, lru_gate_a_b, lru_gate_x_w, lru_gate_x_b, lru_lambda, dn_conv_w,
              dn_A_log, dn_dt_bias, dn_norm, w_branch_lru, w_branch_dn, w_out,
              ffn2_norm, ffn2_w_in, ffn2_w_out, final_norm):
    for l in range(DEPTH):
        x = x + 0.5 * swiglu_ffn(rms_norm(x, ffn1_norm[l]), ffn1_w_in[l], ffn1_w_out[l])
        x = x + hybrid_mixer(rms_norm(x, mix_norm[l]), w_in[l], lru_conv_w[l], lru_conv_b[l],
                             lru_gate_a_w[l], lru_gate_a_b[l], lru_gate_x_w[l], lru_gate_x_b[l],
                             lru_lambda[l], dn_conv_w[l], dn_A_log[l], dn_dt_bias[l], dn_norm[l],
                             w_branch_lru[l], w_branch_dn[l], w_out[l])
        x = x + 0.5 * swiglu_ffn(rms_norm(x, ffn2_norm[l]), ffn2_w_in[l], ffn2_w_out[l])
    return rms_norm(x, final_norm)
```

```python
import functools
import math

import numpy as np
import jax
import jax.numpy as jnp
from jax import lax
from jax.experimental import pallas as pl
from jax.experimental.pallas import tpu as pltpu

NORM_EPS = 1e-6
CHUNK = 64
CONV_W = 4
LRU_C = 8.0
LANES = 128
SUBLANES = 8
VMEM_LIMIT = 56 * 2**20

f32 = jnp.float32
bf16 = jnp.bfloat16


def _cparams(*sem):
    return pltpu.CompilerParams(dimension_semantics=sem, vmem_limit_bytes=VMEM_LIMIT)


def _tile(n, pref):
    t = min(n, pref)
    assert n % t == 0, (n, pref)
    return t


def _sigmoid(x):
    return 1.0 / (1.0 + jnp.exp(-x))


def _softplus(x):
    return jnp.maximum(x, 0.0) + jnp.log1p(jnp.exp(-jnp.abs(x)))


def _rms(x):
    return x * lax.rsqrt(jnp.mean(x * x, axis=-1, keepdims=True) + NORM_EPS)


def _dot(a, b):
    return jnp.dot(a, b, preferred_element_type=f32)


def _dot_nt(a, b):
    return lax.dot_general(a, b, (((1,), (1,)), ((), ())), preferred_element_type=f32)


def _dot_tn(a, b):
    return lax.dot_general(a, b, (((0,), (0,)), ((), ())), preferred_element_type=f32)


def _ffn_kernel(x_ref, g_ref, wg_ref, wu_ref, wo_ref, o_ref, h_ref, *, scale):
    @pl.when(pl.program_id(1) == 0)
    def _():
        x = x_ref[...]
        h_ref[...] = (_rms(x) * g_ref[...]).astype(bf16)
        o_ref[...] = x

    h = h_ref[...]
    gate = _dot(h, wg_ref[...])
    up = _dot(h, wu_ref[...])
    act = (gate * _sigmoid(gate) * up).astype(bf16)
    o_ref[...] += scale * _dot(act, wo_ref[...])


def _ffn(x, g, w_in, w_out, l, scale):
    M, D = x.shape
    F = w_out.shape[1]
    tm, tf = _tile(M, 512), _tile(F, 512)
    nf = F // tf
    return pl.pallas_call(
        functools.partial(_ffn_kernel, scale=scale),
        grid=(M // tm, nf),
        in_specs=[
            pl.BlockSpec((tm, D), lambda i, j: (i, 0)),
            pl.BlockSpec((None, 1, D), lambda i, j: (l, 0, 0)),
            pl.BlockSpec((None, D, tf), lambda i, j: (l, 0, j)),
            pl.BlockSpec((None, D, tf), lambda i, j: (l, 0, j + nf)),
            pl.BlockSpec((None, tf, D), lambda i, j: (l, j, 0)),
        ],
        out_specs=pl.BlockSpec((tm, D), lambda i, j: (i, 0)),
        out_shape=jax.ShapeDtypeStruct((M, D), f32),
        scratch_shapes=[pltpu.VMEM((tm, D), bf16)],
        compiler_params=_cparams("parallel", "arbitrary"),
        name="ffn",
    )(x, g, w_in, w_in, w_out)


def _norm_proj_kernel(x_ref, g_ref, w_ref, o_ref, h_ref):
    @pl.when(pl.program_id(1) == 0)
    def _():
        h_ref[...] = (_rms(x_ref[...]) * g_ref[...]).astype(bf16)

    o_ref[...] = _dot(h_ref[...], w_ref[...])


def _norm_proj(x, g, w, l, name):
    M, D = x.shape
    N = w.shape[2]
    tm, tn = _tile(M, 1024), _tile(N, 512)
    return pl.pallas_call(
        _norm_proj_kernel,
        grid=(M // tm, N // tn),
        in_specs=[
            pl.BlockSpec((tm, D), lambda i, j: (i, 0)),
            pl.BlockSpec((None, 1, D), lambda i, j: (l, 0, 0)),
            pl.BlockSpec((None, D, tn), lambda i, j: (l, 0, j)),
        ],
        out_specs=pl.BlockSpec((tm, tn), lambda i, j: (i, j)),
        out_shape=jax.ShapeDtypeStruct((M, N), f32),
        scratch_shapes=[pltpu.VMEM((tm, D), bf16)],
        compiler_params=_cparams("parallel", "arbitrary"),
        name=name,
    )(x, g, w)


def _lru_kernel(xr_ref, yr_ref, cw_ref, cb_ref, wa_ref, ba_ref, wx_ref, bx_ref, lam_ref,
                o_ref, xe_ref, hc_ref, a_ref, u_ref, *, tt, tc):
    t = pl.program_id(2)

    @pl.when(t == 0)
    def _():
        xe_ref[0:SUBLANES, :] = jnp.zeros((SUBLANES, tc), f32)
        hc_ref[...] = jnp.zeros_like(hc_ref)

    @pl.when(t > 0)
    def _():
        xe_ref[0:SUBLANES, :] = xe_ref[tt:tt + SUBLANES, :]

    x = xr_ref[0]
    xe_ref[SUBLANES:SUBLANES + tt, :] = x
    cw = cw_ref[...]
    xc = cw[3:4] * x + cb_ref[...]
    for s in range(1, CONV_W):
        xc = xc + cw[3 - s:4 - s] * xe_ref[SUBLANES - s:SUBLANES - s + tt, :]

    ra, ri = [], []
    for blk in range(tc // LANES):
        xb = xc[:, blk * LANES:(blk + 1) * LANES].astype(bf16)
        ra.append(_dot(xb, wa_ref[blk]))
        ri.append(_dot(xb, wx_ref[blk]))
    r = _sigmoid(jnp.concatenate(ra, axis=1) + ba_ref[...])
    i = _sigmoid(jnp.concatenate(ri, axis=1) + bx_ref[...])
    log_a = (-LRU_C) * r * _softplus(-lam_ref[...])
    a = jnp.exp(log_a)
    u = xc * i * jnp.sqrt(-jnp.tanh(log_a) * (a * a + 1.0))
    a_ref[...] = a
    u_ref[...] = u

    row = lax.broadcasted_iota(jnp.int32, (SUBLANES, tc), 0)

    def group(g, h):
        r0 = pl.multiple_of(g * SUBLANES, SUBLANES)
        A = a_ref[pl.ds(r0, SUBLANES), :]
        B = u_ref[pl.ds(r0, SUBLANES), :]
        for s in (1, 2, 4):
            keep = row >= s
            B = jnp.where(keep, A * pltpu.roll(B, s, 0) + B, B)
            A = jnp.where(keep, A * pltpu.roll(A, s, 0), A)
        H = A * h + B
        u_ref[pl.ds(r0, SUBLANES), :] = H
        return H[SUBLANES - 1:SUBLANES, :]

    hc_ref[0:1, :] = lax.fori_loop(0, tt // SUBLANES, group, hc_ref[0:1, :])

    y = yr_ref[0]
    gelu = 0.5 * y * (1.0 + jnp.tanh(math.sqrt(2.0 / math.pi) * (y + 0.044715 * (y * y * y))))
    o_ref[0] = (u_ref[...] * gelu).astype(bf16)


def _lru(P, cw, cb, wa, ba, wx, bx, lam, l, R):
    B, T, _ = P.shape
    tt, tc = _tile(T, 512), _tile(R, 512)
    nc = R // tc
    nb = tc // LANES
    vec = lambda: pl.BlockSpec((None, 1, tc), lambda b, c, t: (l, 0, c))
    gw = lambda: pl.BlockSpec((None, nb, LANES, LANES), lambda b, c, t: (l, c, 0, 0))
    return pl.pallas_call(
        functools.partial(_lru_kernel, tt=tt, tc=tc),
        grid=(B, nc, T // tt),
        in_specs=[
            pl.BlockSpec((1, tt, tc), lambda b, c, t: (b, t, c)),
            pl.BlockSpec((1, tt, tc), lambda b, c, t: (b, t, c + nc)),
            pl.BlockSpec((None, CONV_W, tc), lambda b, c, t: (l, 0, c)),
            vec(), gw(), vec(), gw(), vec(), vec(),
        ],
        out_specs=pl.BlockSpec((1, tt, tc), lambda b, c, t: (b, t, c)),
        out_shape=jax.ShapeDtypeStruct((B, T, R), bf16),
        scratch_shapes=[
            pltpu.VMEM((tt + SUBLANES, tc), f32),
            pltpu.VMEM((SUBLANES, tc), f32),
            pltpu.VMEM((tt, tc), f32),
            pltpu.VMEM((tt, tc), f32),
        ],
        compiler_params=_cparams("parallel", "parallel", "arbitrary"),
        name="lru",
    )(P, P, cw, cb, wa, ba, wx, bx, lam)


def _dn_gate_kernel(ba_ref, alog_ref, dtb_ref, tril_ref, pcol_ref, prow_ref, gcol_ref, grow_ref, *, vh):
    ba = ba_ref[0]
    lane = lax.broadcasted_iota(jnp.int32, ba.shape, 1)
    beta = _sigmoid(ba)
    g = -jnp.exp(alog_ref[...]) * _softplus(ba + dtb_ref[...])
    gcum = jnp.dot(tril_ref[...], g, preferred_element_type=f32, precision=lax.Precision.HIGHEST)
    vals = jnp.where(lane < vh, beta, gcum)
    gcol_ref[0] = jnp.dot(vals, pcol_ref[...], preferred_element_type=f32, precision=lax.Precision.HIGHEST)
    grow_ref[0] = lax.dot_general(prow_ref[...], vals, (((1,), (1,)), ((), ())),
                                  preferred_element_type=f32, precision=lax.Precision.HIGHEST)


def _dn_gate_consts(vh, tt):
    kh = vh // 2
    pcol = np.zeros((LANES, kh * LANES), np.float32)
    prow = np.zeros((kh * SUBLANES, LANES), np.float32)
    for p in range(kh):
        for m, src in enumerate((2 * p, 2 * p + 1, vh + 2 * p, vh + 2 * p + 1)):
            pcol[src, p * LANES + m] = 1.0
            prow[p * SUBLANES + m, src] = 1.0
    idx = np.arange(tt)
    tril = ((idx[:, None] // CHUNK == idx[None, :] // CHUNK) & (idx[None, :] <= idx[:, None])).astype(np.float32)
    return jnp.asarray(tril), jnp.asarray(pcol), jnp.asarray(prow)


def _dn_gates(ba, alog, dtb, l, vh):
    B, T, _ = ba.shape
    kh = vh // 2
    tt = _tile(T, 256)
    tril, pcol, prow = _dn_gate_consts(vh, tt)
    full = lambda a: pl.BlockSpec(a.shape, lambda b, t: (0,) * a.ndim)
    return pl.pallas_call(
        functools.partial(_dn_gate_kernel, vh=vh),
        grid=(B, T // tt),
        in_specs=[
            pl.BlockSpec((1, tt, LANES), lambda b, t: (b, t, 0)),
            pl.BlockSpec((None, 1, LANES), lambda b, t: (l, 0, 0)),
            pl.BlockSpec((None, 1, LANES), lambda b, t: (l, 0, 0)),
            full(tril), full(pcol), full(prow),
        ],
        out_specs=[
            pl.BlockSpec((1, tt, kh * LANES), lambda b, t: (b, t, 0)),
            pl.BlockSpec((1, kh * SUBLANES, tt), lambda b, t: (b, 0, t)),
        ],
        out_shape=[
            jax.ShapeDtypeStruct((B, T, kh * LANES), f32),
            jax.ShapeDtypeStruct((B, kh * SUBLANES, T), f32),
        ],
        compiler_params=_cparams("parallel", "parallel"),
        name="dn_gates",
    )(ba, alog, dtb, tril, pcol, prow)


def _conv_silu(x_ref, w_ref, xe_ref, t, tt):
    width = x_ref.shape[-1]

    @pl.when(t == 0)
    def _():
        xe_ref[0:SUBLANES, :] = jnp.zeros((SUBLANES, width), f32)

    @pl.when(t > 0)
    def _():
        xe_ref[0:SUBLANES, :] = xe_ref[tt:tt + SUBLANES, :]

    x = x_ref[0]
    xe_ref[SUBLANES:SUBLANES + tt, :] = x
    w = w_ref[...]
    y = w[3:4] * x
    for s in range(1, CONV_W):
        y = y + w[3 - s:4 - s] * xe_ref[SUBLANES - s:SUBLANES - s + tt, :]
    return y * _sigmoid(y)


def _unit_lower_inverse(L):
    n = L.shape[0]
    eye = (lax.broadcasted_iota(jnp.int32, (n, n), 0) == lax.broadcasted_iota(jnp.int32, (n, n), 1)).astype(f32)
    P = -L
    X = eye + P
    Pb = P.astype(bf16)
    P = _dot(Pb, Pb)
    factors = int(math.log2(n)) - 1
    for k in range(1, factors + 1):
        Pb = P.astype(bf16)
        if k < factors:
            XP = _dot(jnp.concatenate([X, P], axis=0).astype(bf16), Pb)
            X = X + XP[:n]
            P = XP[n:]
        else:
            X = X + _dot(X.astype(bf16), Pb)
    return X


def _gdn_kernel(q_ref, k_ref, v_ref, z_ref, wq_ref, wk_ref, wv_ref, gcol_ref, grow_ref, nrm_ref,
                o_ref, qe_ref, ke_ref, ve_ref, s_ref, *, tt):
    t = pl.program_id(2)

    @pl.when(t == 0)
    def _():
        s_ref[...] = jnp.zeros_like(s_ref)

    q = _conv_silu(q_ref, wq_ref, qe_ref, t, tt)
    k = _conv_silu(k_ref, wk_ref, ke_ref, t, tt)
    v = _conv_silu(v_ref, wv_ref, ve_ref, t, tt)
    q = q * lax.rsqrt(jnp.sum(q * q, axis=-1, keepdims=True) + NORM_EPS) * (LANES ** -0.5)
    k = k * lax.rsqrt(jnp.sum(k * k, axis=-1, keepdims=True) + NORM_EPS)

    ri = lax.broadcasted_iota(jnp.int32, (CHUNK, CHUNK), 0)
    ci = lax.broadcasted_iota(jnp.int32, (CHUNK, CHUNK), 1)
    causal = ci <= ri
    strict = ci < ri
    nrm = nrm_ref[...]

    for c in range(tt // CHUNK):
        r0 = c * CHUNK
        qc = q[r0:r0 + CHUNK]
        kc = k[r0:r0 + CHUNK]
        kcb = kc.astype(bf16)
        kk = _dot_nt(kcb, kcb)
        qk = _dot_nt(qc.astype(bf16), kcb)
        for j in range(2):
            beta = gcol_ref[0, r0:r0 + CHUNK, j:j + 1]
            gc = gcol_ref[0, r0:r0 + CHUNK, 2 + j:3 + j]
            gr = grow_ref[0, 2 + j:3 + j, r0:r0 + CHUNK]
            decay = jnp.where(causal, jnp.exp(gc - gr), 0.0)
            L = jnp.where(strict, beta * kk * decay, 0.0)
            Tm = _unit_lower_inverse(L)
            eg = jnp.exp(gc)
            gl = gc[CHUNK - 1:CHUNK, :]
            vc = v[r0:r0 + CHUNK, j * LANES:(j + 1) * LANES]
            rhs = jnp.concatenate([vc * beta, kc * (beta * eg)], axis=1).astype(bf16)
            sol = _dot(Tm.astype(bf16), rhs)
            u, w = sol[:, :LANES], sol[:, LANES:]
            aqk = (qk * decay).astype(bf16)
            qd = qc * eg
            kd = (kc * jnp.exp(gl - gc)).astype(bf16)
            S = s_ref[j]
            wq_s = _dot(jnp.concatenate([w, qd], axis=0).astype(bf16), S.astype(bf16))
            v_new = (u - wq_s[:CHUNK]).astype(bf16)
            o = wq_s[CHUNK:] + _dot(aqk, v_new)
            s_ref[j] = S * jnp.exp(gl) + _dot_tn(kd, v_new)
            zc = z_ref[0, r0:r0 + CHUNK, j * LANES:(j + 1) * LANES]
            y = _rms(o) * nrm * (zc * _sigmoid(zc))
            o_ref[0, r0:r0 + CHUNK, j * LANES:(j + 1) * LANES] = y.astype(bf16)


def _gdn(P, cw, gcol, grow, nrm, l, offs):
    B, T, _ = P.shape
    off_q, off_k, off_v, off_z, Q, V = offs
    KH = Q // LANES
    tt = _tile(T, 256)
    hw = 2 * LANES
    cq, ck, cv, cz = off_q // LANES, off_k // LANES, off_v // hw, off_z // hw
    wk0, wv0 = Q // LANES, (2 * Q) // hw
    return pl.pallas_call(
        functools.partial(_gdn_kernel, tt=tt),
        grid=(B, KH, T // tt),
        in_specs=[
            pl.BlockSpec((1, tt, LANES), lambda b, h, t: (b, t, cq + h)),
            pl.BlockSpec((1, tt, LANES), lambda b, h, t: (b, t, ck + h)),
            pl.BlockSpec((1, tt, hw), lambda b, h, t: (b, t, cv + h)),
            pl.BlockSpec((1, tt, hw), lambda b, h, t: (b, t, cz + h)),
            pl.BlockSpec((None, CONV_W, LANES), lambda b, h, t: (l, 0, h)),
            pl.BlockSpec((None, CONV_W, LANES), lambda b, h, t: (l, 0, wk0 + h)),
            pl.BlockSpec((None, CONV_W, hw), lambda b, h, t: (l, 0, wv0 + h)),
            pl.BlockSpec((1, tt, LANES), lambda b, h, t: (b, t, h)),
            pl.BlockSpec((1, SUBLANES, tt), lambda b, h, t: (b, h, t)),
            pl.BlockSpec((None, 1, LANES), lambda b, h, t: (l, 0, 0)),
        ],
        out_specs=pl.BlockSpec((1, tt, hw), lambda b, h, t: (b, t, h)),
        out_shape=jax.ShapeDtypeStruct((B, T, V), bf16),
        scratch_shapes=[
            pltpu.VMEM((tt + SUBLANES, LANES), f32),
            pltpu.VMEM((tt + SUBLANES, LANES), f32),
            pltpu.VMEM((tt + SUBLANES, hw), f32),
            pltpu.VMEM((2, LANES, LANES), f32),
        ],
        compiler_params=_cparams("parallel", "parallel", "arbitrary"),
        name="gdn",
    )(P, P, P, P, cw, cw, cw, gcol, grow, nrm)


def _merge_kernel(x_ref, yl_ref, yd_ref, gl_ref, gd_ref, wl_ref, wd_ref, wo_ref, o_ref):
    @pl.when(pl.program_id(1) == 0)
    def _():
        o_ref[...] = x_ref[...]

    a = _dot(yl_ref[...], wl_ref[...])
    b = _dot(yd_ref[...], wd_ref[...])
    merged = (_sigmoid(gl_ref[...]) * a + _sigmoid(gd_ref[...]) * b).astype(bf16)
    o_ref[...] += _dot(merged, wo_ref[...])


def _merge(x, yl, yd, P, wl, wd, wo, l, off_g):
    M, D = x.shape
    R, V = yl.shape[1], yd.shape[1]
    tm, tn = _tile(M, 512), _tile(D, 256)
    cg = off_g // tn
    nd = D // tn
    return pl.pallas_call(
        _merge_kernel,
        grid=(M // tm, nd),
        in_specs=[
            pl.BlockSpec((tm, D), lambda i, j: (i, 0)),
            pl.BlockSpec((tm, R), lambda i, j: (i, 0)),
            pl.BlockSpec((tm, V), lambda i, j: (i, 0)),
            pl.BlockSpec((tm, tn), lambda i, j: (i, cg + j)),
            pl.BlockSpec((tm, tn), lambda i, j: (i, cg + nd + j)),
            pl.BlockSpec((None, R, tn), lambda i, j: (l, 0, j)),
            pl.BlockSpec((None, V, tn), lambda i, j: (l, 0, j)),
            pl.BlockSpec((None, tn, D), lambda i, j: (l, j, 0)),
        ],
        out_specs=pl.BlockSpec((tm, D), lambda i, j: (i, 0)),
        out_shape=jax.ShapeDtypeStruct((M, D), f32),
        compiler_params=_cparams("parallel", "arbitrary"),
        name="merge",
    )(x, yl, yd, P, P, wl, wd, wo)


def _final_norm_kernel(x_ref, g_ref, o_ref):
    o_ref[...] = _rms(x_ref[...]) * g_ref[...]


def _final_norm(x, g):
    M, D = x.shape
    tm = _tile(M, 512)
    return pl.pallas_call(
        _final_norm_kernel,
        grid=(M // tm,),
        in_specs=[pl.BlockSpec((tm, D), lambda i: (i, 0)), pl.BlockSpec((1, D), lambda i: (0, 0))],
        out_specs=pl.BlockSpec((tm, D), lambda i: (i, 0)),
        out_shape=jax.ShapeDtypeStruct((M, D), f32),
        compiler_params=_cparams("parallel"),
        name="final_norm",
    )(x, g)


def kernel(x, ffn1_norm, ffn1_w_in, ffn1_w_out, mix_norm, w_in, lru_conv_w, lru_conv_b, lru_gate_a_w, lru_gate_a_b, lru_gate_x_w, lru_gate_x_b, lru_lambda, dn_conv_w, dn_A_log, dn_dt_bias, dn_norm, w_branch_lru, w_branch_dn, w_out, ffn2_norm, ffn2_w_in, ffn2_w_out, final_norm):
    B, T, D = x.shape
    depth = ffn1_norm.shape[0]
    R = lru_conv_w.shape[-1]
    VH = dn_A_log.shape[-1]
    V = VH * LANES
    Q = (dn_conv_w.shape[-1] - V) // 2
    assert Q // LANES == VH // 2 and 2 * VH <= LANES and T % CHUNK == 0
    off_q = 2 * R
    off_k = off_q + Q
    off_v = off_k + Q
    off_z = off_v + V
    off_b = off_z + V
    off_gates = off_b + 2 * VH
    assert w_in.shape[-1] == off_gates + 2 * D

    row = lambda p: p[:, None, :]
    w_main = jnp.concatenate([w_in[:, :, :off_b], w_in[:, :, off_gates:]], axis=-1).astype(bf16)
    w_ba = jnp.pad(w_in[:, :, off_b:off_gates], ((0, 0), (0, 0), (0, LANES - 2 * VH))).astype(bf16)
    ffn1_wi, ffn1_wo = ffn1_w_in.astype(bf16), ffn1_w_out.astype(bf16)
    ffn2_wi, ffn2_wo = ffn2_w_in.astype(bf16), ffn2_w_out.astype(bf16)
    wl, wd, wo = w_branch_lru.astype(bf16), w_branch_dn.astype(bf16), w_out.astype(bf16)
    wa, wx = lru_gate_a_w.astype(bf16), lru_gate_x_w.astype(bf16)
    pad_a = lambda p: jnp.pad(p, ((0, 0), (VH, LANES - 2 * VH)))[:, None, :]
    alog, dtb = pad_a(dn_A_log), pad_a(dn_dt_bias)

    xm = x.reshape(B * T, D)
    for l in range(depth):
        xm = _ffn(xm, row(ffn1_norm), ffn1_wi, ffn1_wo, l, 0.5)
        P = _norm_proj(xm, row(mix_norm), w_main, l, "in_proj").reshape(B, T, -1)
        ba = _norm_proj(xm, row(mix_norm), w_ba, l, "ba_proj").reshape(B, T, LANES)
        y_lru = _lru(P, lru_conv_w, row(lru_conv_b), wa, row(lru_gate_a_b), wx, row(lru_gate_x_b),
                     row(lru_lambda), l, R)
        gcol, grow = _dn_gates(ba, alog, dtb, l, VH)
        y_dn = _gdn(P, dn_conv_w, gcol, grow, row(dn_norm), l, (off_q, off_k, off_v, off_z, Q, V))
        xm = _merge(xm, y_lru.reshape(B * T, R), y_dn.reshape(B * T, V), P.reshape(B * T, -1),
                    wl, wd, wo, l, off_b)
        xm = _ffn(xm, row(ffn2_norm), ffn2_wi, ffn2_wo, l, 0.5)
    return _final_norm(xm, final_norm[None, :]).reshape(B, T, D)
```

```python
import functools
import math

import numpy as np
import jax
import jax.numpy as jnp
from jax import lax
from jax.experimental import pallas as pl
from jax.experimental.pallas import tpu as pltpu

NORM_EPS = 1e-6
CHUNK = 64
PAIR = 2 * CHUNK
CONV_W = 4
LRU_C = 8.0
LANES = 128
SUBLANES = 8
VMEM_LIMIT = 56 * 2**20

f32 = jnp.float32
bf16 = jnp.bfloat16


def _cparams(*sem):
    return pltpu.CompilerParams(dimension_semantics=sem, vmem_limit_bytes=VMEM_LIMIT)


def _tile(n, pref):
    t = min(n, pref)
    assert n % t == 0, (n, pref)
    return t


def _sigmoid(x):
    return 1.0 / (1.0 + jnp.exp(-x))


def _softplus(x):
    return jnp.maximum(x, 0.0) + jnp.log1p(jnp.exp(-jnp.abs(x)))


def _rms(x):
    return x * lax.rsqrt(jnp.mean(x * x, axis=-1, keepdims=True) + NORM_EPS)


def _dot(a, b):
    return jnp.dot(a, b, preferred_element_type=f32)


def _dot_nt(a, b):
    return lax.dot_general(a, b, (((1,), (1,)), ((), ())), preferred_element_type=f32)


def _dot_tn(a, b):
    return lax.dot_general(a, b, (((0,), (0,)), ((), ())), preferred_element_type=f32)


def _ffn_kernel(x_ref, g_ref, wg_ref, wu_ref, wo_ref, o_ref, h_ref, *, scale):
    @pl.when(pl.program_id(1) == 0)
    def _():
        x = x_ref[...]
        h_ref[...] = (_rms(x) * g_ref[...]).astype(bf16)
        o_ref[...] = x

    h = h_ref[...]
    gate = _dot(h, wg_ref[...])
    up = _dot(h, wu_ref[...])
    act = (gate * _sigmoid(gate) * up).astype(bf16)
    o_ref[...] += scale * _dot(act, wo_ref[...])


def _ffn(x, g, w_in, w_out, l, scale):
    M, D = x.shape
    F = w_out.shape[1]
    tm, tf = _tile(M, 512), _tile(F, 512)
    nf = F // tf
    return pl.pallas_call(
        functools.partial(_ffn_kernel, scale=scale),
        grid=(M // tm, nf),
        in_specs=[
            pl.BlockSpec((tm, D), lambda i, j: (i, 0)),
            pl.BlockSpec((None, 1, D), lambda i, j: (l, 0, 0)),
            pl.BlockSpec((None, D, tf), lambda i, j: (l, 0, j)),
            pl.BlockSpec((None, D, tf), lambda i, j: (l, 0, j + nf)),
            pl.BlockSpec((None, tf, D), lambda i, j: (l, j, 0)),
        ],
        out_specs=pl.BlockSpec((tm, D), lambda i, j: (i, 0)),
        out_shape=jax.ShapeDtypeStruct((M, D), f32),
        scratch_shapes=[pltpu.VMEM((tm, D), bf16)],
        compiler_params=_cparams("parallel", "arbitrary"),
        name="ffn",
    )(x, g, w_in, w_in, w_out)


def _norm_proj_kernel(x_ref, g_ref, w_ref, o_ref, h_ref):
    @pl.when(pl.program_id(1) == 0)
    def _():
        h_ref[...] = (_rms(x_ref[...]) * g_ref[...]).astype(bf16)

    o_ref[...] = _dot(h_ref[...], w_ref[...])


def _norm_proj(x, g, w, l, name):
    M, D = x.shape
    N = w.shape[2]
    tm, tn = _tile(M, 1024), _tile(N, 512)
    return pl.pallas_call(
        _norm_proj_kernel,
        grid=(M // tm, N // tn),
        in_specs=[
            pl.BlockSpec((tm, D), lambda i, j: (i, 0)),
            pl.BlockSpec((None, 1, D), lambda i, j: (l, 0, 0)),
            pl.BlockSpec((None, D, tn), lambda i, j: (l, 0, j)),
        ],
        out_specs=pl.BlockSpec((tm, tn), lambda i, j: (i, j)),
        out_shape=jax.ShapeDtypeStruct((M, N), f32),
        scratch_shapes=[pltpu.VMEM((tm, D), bf16)],
        compiler_params=_cparams("parallel", "arbitrary"),
        name=name,
    )(x, g, w)


def _lru_kernel(xr_ref, yr_ref, cw_ref, cb_ref, wa_ref, ba_ref, wx_ref, bx_ref, lam_ref,
                o_ref, xe_ref, hc_ref, a_ref, u_ref, *, tt, tc):
    t = pl.program_id(2)

    @pl.when(t == 0)
    def _():
        xe_ref[0:SUBLANES, :] = jnp.zeros((SUBLANES, tc), f32)
        hc_ref[...] = jnp.zeros_like(hc_ref)

    @pl.when(t > 0)
    def _():
        xe_ref[0:SUBLANES, :] = xe_ref[tt:tt + SUBLANES, :]

    x = xr_ref[0]
    xe_ref[SUBLANES:SUBLANES + tt, :] = x
    cw = cw_ref[...]
    xc = cw[3:4] * x + cb_ref[...]
    for s in range(1, CONV_W):
        xc = xc + cw[3 - s:4 - s] * xe_ref[SUBLANES - s:SUBLANES - s + tt, :]

    ra, ri = [], []
    for blk in range(tc // LANES):
        xb = xc[:, blk * LANES:(blk + 1) * LANES].astype(bf16)
        ra.append(_dot(xb, wa_ref[blk]))
        ri.append(_dot(xb, wx_ref[blk]))
    r = _sigmoid(jnp.concatenate(ra, axis=1) + ba_ref[...])
    i = _sigmoid(jnp.concatenate(ri, axis=1) + bx_ref[...])
    log_a = (-LRU_C) * r * _softplus(-lam_ref[...])
    a = jnp.exp(log_a)
    u = xc * i * jnp.sqrt(-jnp.tanh(log_a) * (a * a + 1.0))
    a_ref[...] = a
    u_ref[...] = u

    row = lax.broadcasted_iota(jnp.int32, (SUBLANES, tc), 0)

    def group(g, h):
        r0 = pl.multiple_of(g * SUBLANES, SUBLANES)
        A = a_ref[pl.ds(r0, SUBLANES), :]
        B = u_ref[pl.ds(r0, SUBLANES), :]
        for s in (1, 2, 4):
            keep = row >= s
            B = jnp.where(keep, A * pltpu.roll(B, s, 0) + B, B)
            A = jnp.where(keep, A * pltpu.roll(A, s, 0), A)
        H = A * h + B
        u_ref[pl.ds(r0, SUBLANES), :] = H
        return H[SUBLANES - 1:SUBLANES, :]

    hc_ref[0:1, :] = lax.fori_loop(0, tt // SUBLANES, group, hc_ref[0:1, :])

    y = yr_ref[0]
    gelu = 0.5 * y * (1.0 + jnp.tanh(math.sqrt(2.0 / math.pi) * (y + 0.044715 * (y * y * y))))
    o_ref[0] = (u_ref[...] * gelu).astype(bf16)


def _lru(P, cw, cb, wa, ba, wx, bx, lam, l, R):
    B, T, _ = P.shape
    tt, tc = _tile(T, 512), _tile(R, 512)
    nc = R // tc
    nb = tc // LANES
    vec = lambda: pl.BlockSpec((None, 1, tc), lambda b, c, t: (l, 0, c))
    gw = lambda: pl.BlockSpec((None, nb, LANES, LANES), lambda b, c, t: (l, c, 0, 0))
    return pl.pallas_call(
        functools.partial(_lru_kernel, tt=tt, tc=tc),
        grid=(B, nc, T // tt),
        in_specs=[
            pl.BlockSpec((1, tt, tc), lambda b, c, t: (b, t, c)),
            pl.BlockSpec((1, tt, tc), lambda b, c, t: (b, t, c + nc)),
            pl.BlockSpec((None, CONV_W, tc), lambda b, c, t: (l, 0, c)),
            vec(), gw(), vec(), gw(), vec(), vec(),
        ],
        out_specs=pl.BlockSpec((1, tt, tc), lambda b, c, t: (b, t, c)),
        out_shape=jax.ShapeDtypeStruct((B, T, R), bf16),
        scratch_shapes=[
            pltpu.VMEM((tt + SUBLANES, tc), f32),
            pltpu.VMEM((SUBLANES, tc), f32),
            pltpu.VMEM((tt, tc), f32),
            pltpu.VMEM((tt, tc), f32),
        ],
        compiler_params=_cparams("parallel", "parallel", "arbitrary"),
        name="lru",
    )(P, P, cw, cb, wa, ba, wx, bx, lam)


def _dn_gate_kernel(ba_ref, alog_ref, dtb_ref, tril_ref, pcol_ref, prow_ref, gcol_ref, grow_ref, *, vh):
    ba = ba_ref[0]
    lane = lax.broadcasted_iota(jnp.int32, ba.shape, 1)
    beta = _sigmoid(ba)
    g = -jnp.exp(alog_ref[...]) * _softplus(ba + dtb_ref[...])
    gcum = jnp.dot(tril_ref[...], g, preferred_element_type=f32, precision=lax.Precision.HIGHEST)
    vals = jnp.where(lane < vh, beta, gcum)
    gcol_ref[0] = jnp.dot(vals, pcol_ref[...], preferred_element_type=f32, precision=lax.Precision.HIGHEST)
    grow_ref[0] = lax.dot_general(prow_ref[...], vals, (((1,), (1,)), ((), ())),
                                  preferred_element_type=f32, precision=lax.Precision.HIGHEST)


def _dn_gate_consts(vh, tt):
    kh = vh // 2
    pcol = np.zeros((LANES, kh * LANES), np.float32)
    prow = np.zeros((kh * SUBLANES, LANES), np.float32)
    for p in range(kh):
        for m, src in enumerate((2 * p, 2 * p + 1, vh + 2 * p, vh + 2 * p + 1)):
            pcol[src, p * LANES + m] = 1.0
            prow[p * SUBLANES + m, src] = 1.0
    idx = np.arange(tt)
    tril = ((idx[:, None] // CHUNK == idx[None, :] // CHUNK) & (idx[None, :] <= idx[:, None])).astype(np.float32)
    return jnp.asarray(tril), jnp.asarray(pcol), jnp.asarray(prow)


def _dn_gates(ba, alog, dtb, l, vh):
    B, T, _ = ba.shape
    kh = vh // 2
    tt = _tile(T, 256)
    tril, pcol, prow = _dn_gate_consts(vh, tt)
    full = lambda a: pl.BlockSpec(a.shape, lambda b, t: (0,) * a.ndim)
    return pl.pallas_call(
        functools.partial(_dn_gate_kernel, vh=vh),
        grid=(B, T // tt),
        in_specs=[
            pl.BlockSpec((1, tt, LANES), lambda b, t: (b, t, 0)),
            pl.BlockSpec((None, 1, LANES), lambda b, t: (l, 0, 0)),
            pl.BlockSpec((None, 1, LANES), lambda b, t: (l, 0, 0)),
            full(tril), full(pcol), full(prow),
        ],
        out_specs=[
            pl.BlockSpec((1, tt, kh * LANES), lambda b, t: (b, t, 0)),
            pl.BlockSpec((1, kh * SUBLANES, tt), lambda b, t: (b, 0, t)),
        ],
        out_shape=[
            jax.ShapeDtypeStruct((B, T, kh * LANES), f32),
            jax.ShapeDtypeStruct((B, kh * SUBLANES, T), f32),
        ],
        compiler_params=_cparams("parallel", "parallel"),
        name="dn_gates",
    )(ba, alog, dtb, tril, pcol, prow)


def _conv_silu(x_ref, halo_ref, w_ref, xe_ref, t, tt):
    xe_ref[0:SUBLANES, :] = jnp.where(t > 0, halo_ref[0], 0.0)
    x = x_ref[0]
    xe_ref[SUBLANES:SUBLANES + tt, :] = x
    w = w_ref[...]
    y = w[3:4] * x
    for s in range(1, CONV_W):
        y = y + w[3 - s:4 - s] * xe_ref[SUBLANES - s:SUBLANES - s + tt, :]
    return y * _sigmoid(y)


def _gdn_prep_kernel(q_ref, k_ref, v_ref, qh_ref, kh_ref, vh_ref, wq_ref, wk_ref, wv_ref, gcol_ref, grow_ref,
                     u_ref, w_ref, qd_ref, kd_ref, aqk_ref, qe_ref, ke_ref, ve_ref, *, tt):
    t = pl.program_id(2)
    q = _conv_silu(q_ref, qh_ref, wq_ref, qe_ref, t, tt)
    k = _conv_silu(k_ref, kh_ref, wk_ref, ke_ref, t, tt)
    v = _conv_silu(v_ref, vh_ref, wv_ref, ve_ref, t, tt)
    q = q * lax.rsqrt(jnp.sum(q * q, axis=-1, keepdims=True) + NORM_EPS) * (LANES ** -0.5)
    k = k * lax.rsqrt(jnp.sum(k * k, axis=-1, keepdims=True) + NORM_EPS)

    ri = lax.broadcasted_iota(jnp.int32, (PAIR, PAIR), 0)
    ci = lax.broadcasted_iota(jnp.int32, (PAIR, PAIR), 1)
    same_chunk = (ri >= CHUNK) == (ci >= CHUNK)
    causal = same_chunk & (ci <= ri)
    strict = same_chunk & (ci < ri)
    eye = (ri == ci).astype(f32)
    first = lax.broadcasted_iota(jnp.int32, (PAIR, 1), 0) < CHUNK
    levels = int(math.log2(CHUNK)) - 2

    pairs = range(tt // PAIR)
    rows = [slice(p * PAIR, (p + 1) * PAIR) for p in pairs]
    kb = [k[r].astype(bf16) for r in rows]
    kk = [_dot_nt(kb[p], kb[p]) for p in pairs]
    qk = [_dot_nt(q[rows[p]].astype(bf16), kb[p]) for p in pairs]

    chains = [(p, j) for p in pairs for j in range(2)]
    beta = [gcol_ref[0, rows[p], j:j + 1] for p, j in chains]
    gc = [gcol_ref[0, rows[p], 2 + j:3 + j] for p, j in chains]
    decay = [jnp.where(causal, jnp.exp(gc[i] - grow_ref[0, 2 + j:3 + j, rows[p]]), 0.0)
             for i, (p, j) in enumerate(chains)]
    N = [jnp.where(strict, -(beta[i] * kk[p] * decay[i]), 0.0) for i, (p, j) in enumerate(chains)]
    X = [eye + n for n in N]
    Nb = [n.astype(bf16) for n in N]
    P = [_dot(nb, nb) for nb in Nb]
    for _ in range(levels):
        Pb = [m.astype(bf16) for m in P]
        PX = [_dot(Pb[i], jnp.concatenate([Pb[i], X[i].astype(bf16)], axis=1)) for i in range(len(chains))]
        P = [m[:, :PAIR] for m in PX]
        X = [X[i] + PX[i][:, PAIR:] for i in range(len(chains))]
    X = [X[i] + _dot(P[i].astype(bf16), X[i].astype(bf16)) for i in range(len(chains))]

    for i, (p, j) in enumerate(chains):
        qp, kp = q[rows[p]], k[rows[p]]
        eg = jnp.exp(gc[i])
        vp = v[rows[p], j * LANES:(j + 1) * LANES]
        rhs = jnp.concatenate([vp * beta[i], kp * (beta[i] * eg)], axis=1).astype(bf16)
        sol = _dot(X[i].astype(bf16), rhs)
        g_last = jnp.where(first, gc[i][CHUNK - 1:CHUNK, :], gc[i][PAIR - 1:PAIR, :])
        u_ref[0, j, rows[p], :] = sol[:, :LANES]
        w_ref[0, j, rows[p], :] = sol[:, LANES:].astype(bf16)
        qd_ref[0, j, rows[p], :] = (qp * eg).astype(bf16)
        kd_ref[0, j, rows[p], :] = (kp * jnp.exp(g_last - gc[i])).astype(bf16)
        aqk_ref[0, j, rows[p], :] = (qk[p] * decay[i]).astype(bf16)


def _gdn_prep(P, cw, gcol, grow, l, offs):
    B, T, _ = P.shape
    off_q, off_k, off_v, off_z, Q, V = offs
    KH = Q // LANES
    VH = 2 * KH
    tt = _tile(T, 256)
    hw = 2 * LANES
    cq, ck, cv = off_q // LANES, off_k // LANES, off_v // hw
    wk0, wv0 = Q // LANES, (2 * Q) // hw
    hb = tt // SUBLANES
    prev = lambda t: jnp.maximum(t * hb - 1, 0)
    head_out = lambda dt: (pl.BlockSpec((1, 2, tt, LANES), lambda b, h, t: (b, h, t, 0)),
                           jax.ShapeDtypeStruct((B, VH, T, LANES), dt))
    outs = [head_out(f32)] + [head_out(bf16) for _ in range(4)]
    return pl.pallas_call(
        functools.partial(_gdn_prep_kernel, tt=tt),
        grid=(B, KH, T // tt),
        in_specs=[
            pl.BlockSpec((1, tt, LANES), lambda b, h, t: (b, t, cq + h)),
            pl.BlockSpec((1, tt, LANES), lambda b, h, t: (b, t, ck + h)),
            pl.BlockSpec((1, tt, hw), lambda b, h, t: (b, t, cv + h)),
            pl.BlockSpec((1, SUBLANES, LANES), lambda b, h, t: (b, prev(t), cq + h)),
            pl.BlockSpec((1, SUBLANES, LANES), lambda b, h, t: (b, prev(t), ck + h)),
            pl.BlockSpec((1, SUBLANES, hw), lambda b, h, t: (b, prev(t), cv + h)),
            pl.BlockSpec((None, CONV_W, LANES), lambda b, h, t: (l, 0, h)),
            pl.BlockSpec((None, CONV_W, LANES), lambda b, h, t: (l, 0, wk0 + h)),
            pl.BlockSpec((None, CONV_W, hw), lambda b, h, t: (l, 0, wv0 + h)),
            pl.BlockSpec((1, tt, LANES), lambda b, h, t: (b, t, h)),
            pl.BlockSpec((1, SUBLANES, tt), lambda b, h, t: (b, h, t)),
        ],
        out_specs=[o[0] for o in outs],
        out_shape=[o[1] for o in outs],
        scratch_shapes=[
            pltpu.VMEM((tt + SUBLANES, LANES), f32),
            pltpu.VMEM((tt + SUBLANES, LANES), f32),
            pltpu.VMEM((tt + SUBLANES, hw), f32),
        ],
        compiler_params=_cparams("parallel", "parallel", "parallel"),
        name="gdn_prep",
    )(P, P, P, P, P, P, cw, cw, cw, gcol, grow)


def _gdn_scan_kernel(u_ref, w_ref, qd_ref, kd_ref, aqk_ref, z_ref, gcol_ref, nrm_ref, o_ref, s_ref, *, tt, nh):
    @pl.when(pl.program_id(2) == 0)
    def _():
        s_ref[...] = jnp.zeros_like(s_ref)

    nrm = nrm_ref[...]
    zeros = jnp.zeros((CHUNK, LANES), bf16)
    heads = range(nh)
    S = [s_ref[g] for g in heads]
    v_prev = [zeros for _ in heads]
    for c in range(tt // CHUNK):
        rows = slice(c * CHUNK, (c + 1) * CHUNK)
        ws_qs = [_dot(jnp.concatenate([w_ref[0, g, rows, :], qd_ref[0, g, rows, :]], axis=0),
                      S[g].astype(bf16)) for g in heads]
        v_new = [(u_ref[0, g, rows, :] - ws_qs[g][:CHUNK]).astype(bf16) for g in heads]
        v_pair = [jnp.concatenate([v_new[g], zeros] if c % 2 == 0 else [v_prev[g], v_new[g]], axis=0)
                  for g in heads]
        o = [ws_qs[g][CHUNK:] + _dot(aqk_ref[0, g, rows, :], v_pair[g]) for g in heads]
        for g in heads:
            lane_g = (g // 2) * LANES + 2 + g % 2
            g_last = gcol_ref[0, (c + 1) * CHUNK - 1:(c + 1) * CHUNK, lane_g:lane_g + 1]
            S[g] = S[g] * jnp.exp(g_last) + _dot_tn(kd_ref[0, g, rows, :], v_new[g])
        for g in heads:
            z = z_ref[0, rows, g * LANES:(g + 1) * LANES]
            o_ref[0, rows, g * LANES:(g + 1) * LANES] = (_rms(o[g]) * nrm * (z * _sigmoid(z))).astype(bf16)
        v_prev = v_new
    for g in heads:
        s_ref[g] = S[g]


def _gdn_scan(factors, P, gcol, nrm, l, off_z):
    u = factors[0]
    B, VH, T, _ = u.shape
    nh = _tile(VH, 8)
    tt = _tile(T, 256)
    cz = off_z // (nh * LANES)
    assert off_z % (nh * LANES) == 0 and nh % 2 == 0
    head_in = lambda: pl.BlockSpec((1, nh, tt, LANES), lambda b, h, t: (b, h, t, 0))
    return pl.pallas_call(
        functools.partial(_gdn_scan_kernel, tt=tt, nh=nh),
        grid=(B, VH // nh, T // tt),
        in_specs=[
            head_in(), head_in(), head_in(), head_in(), head_in(),
            pl.BlockSpec((1, tt, nh * LANES), lambda b, h, t: (b, t, cz + h)),
            pl.BlockSpec((1, tt, (nh // 2) * LANES), lambda b, h, t: (b, t, h)),
            pl.BlockSpec((None, 1, LANES), lambda b, h, t: (l, 0, 0)),
        ],
        out_specs=pl.BlockSpec((1, tt, nh * LANES), lambda b, h, t: (b, t, h)),
        out_shape=jax.ShapeDtypeStruct((B, T, VH * LANES), bf16),
        scratch_shapes=[pltpu.VMEM((nh, LANES, LANES), f32)],
        compiler_params=_cparams("parallel", "parallel", "arbitrary"),
        name="gdn_scan",
    )(*factors, P, gcol, nrm)


def _merge_kernel(x_ref, yl_ref, yd_ref, gl_ref, gd_ref, wl_ref, wd_ref, wo_ref, o_ref):
    @pl.when(pl.program_id(1) == 0)
    def _():
        o_ref[...] = x_ref[...]

    a = _dot(yl_ref[...], wl_ref[...])
    b = _dot(yd_ref[...], wd_ref[...])
    merged = (_sigmoid(gl_ref[...]) * a + _sigmoid(gd_ref[...]) * b).astype(bf16)
    o_ref[...] += _dot(merged, wo_ref[...])


def _merge(x, yl, yd, P, wl, wd, wo, l, off_g):
    M, D = x.shape
    R, V = yl.shape[1], yd.shape[1]
    tm, tn = _tile(M, 512), _tile(D, 256)
    cg = off_g // tn
    nd = D // tn
    return pl.pallas_call(
        _merge_kernel,
        grid=(M // tm, nd),
        in_specs=[
            pl.BlockSpec((tm, D), lambda i, j: (i, 0)),
            pl.BlockSpec((tm, R), lambda i, j: (i, 0)),
            pl.BlockSpec((tm, V), lambda i, j: (i, 0)),
            pl.BlockSpec((tm, tn), lambda i, j: (i, cg + j)),
            pl.BlockSpec((tm, tn), lambda i, j: (i, cg + nd + j)),
            pl.BlockSpec((None, R, tn), lambda i, j: (l, 0, j)),
            pl.BlockSpec((None, V, tn), lambda i, j: (l, 0, j)),
            pl.BlockSpec((None, tn, D), lambda i, j: (l, j, 0)),
        ],
        out_specs=pl.BlockSpec((tm, D), lambda i, j: (i, 0)),
        out_shape=jax.ShapeDtypeStruct((M, D), f32),
        compiler_params=_cparams("parallel", "arbitrary"),
        name="merge",
    )(x, yl, yd, P, P, wl, wd, wo)


def _final_norm_kernel(x_ref, g_ref, o_ref):
    o_ref[...] = _rms(x_ref[...]) * g_ref[...]


def _final_norm(x, g):
    M, D = x.shape
    tm = _tile(M, 512)
    return pl.pallas_call(
        _final_norm_kernel,
        grid=(M // tm,),
        in_specs=[pl.BlockSpec((tm, D), lambda i: (i, 0)), pl.BlockSpec((1, D), lambda i: (0, 0))],
        out_specs=pl.BlockSpec((tm, D), lambda i: (i, 0)),
        out_shape=jax.ShapeDtypeStruct((M, D), f32),
        compiler_params=_cparams("parallel"),
        name="final_norm",
    )(x, g)


def kernel(x, ffn1_norm, ffn1_w_in, ffn1_w_out, mix_norm, w_in, lru_conv_w, lru_conv_b, lru_gate_a_w, lru_gate_a_b, lru_gate_x_w, lru_gate_x_b, lru_lambda, dn_conv_w, dn_A_log, dn_dt_bias, dn_norm, w_branch_lru, w_branch_dn, w_out, ffn2_norm, ffn2_w_in, ffn2_w_out, final_norm):
    B, T, D = x.shape
    depth = ffn1_norm.shape[0]
    R = lru_conv_w.shape[-1]
    VH = dn_A_log.shape[-1]
    V = VH * LANES
    Q = (dn_conv_w.shape[-1] - V) // 2
    assert Q // LANES == VH // 2 and 2 * VH <= LANES and T % PAIR == 0
    off_q = 2 * R
    off_k = off_q + Q
    off_v = off_k + Q
    off_z = off_v + V
    off_b = off_z + V
    off_gates = off_b + 2 * VH
    assert w_in.shape[-1] == off_gates + 2 * D

    row = lambda p: p[:, None, :]
    w_main = jnp.concatenate([w_in[:, :, :off_b], w_in[:, :, off_gates:]], axis=-1).astype(bf16)
    w_ba = jnp.pad(w_in[:, :, off_b:off_gates], ((0, 0), (0, 0), (0, LANES - 2 * VH))).astype(bf16)
    ffn1_wi, ffn1_wo = ffn1_w_in.astype(bf16), ffn1_w_out.astype(bf16)
    ffn2_wi, ffn2_wo = ffn2_w_in.astype(bf16), ffn2_w_out.astype(bf16)
    wl, wd, wo = w_branch_lru.astype(bf16), w_branch_dn.astype(bf16), w_out.astype(bf16)
    wa, wx = lru_gate_a_w.astype(bf16), lru_gate_x_w.astype(bf16)
    pad_a = lambda p: jnp.pad(p, ((0, 0), (VH, LANES - 2 * VH)))[:, None, :]
    alog, dtb = pad_a(dn_A_log), pad_a(dn_dt_bias)

    xm = x.reshape(B * T, D)
    for l in range(depth):
        xm = _ffn(xm, row(ffn1_norm), ffn1_wi, ffn1_wo, l, 0.5)
        P = _norm_proj(xm, row(mix_norm), w_main, l, "in_proj").reshape(B, T, -1)
        ba = _norm_proj(xm, row(mix_norm), w_ba, l, "ba_proj").reshape(B, T, LANES)
        y_lru = _lru(P, lru_conv_w, row(lru_conv_b), wa, row(lru_gate_a_b), wx, row(lru_gate_x_b),
                     row(lru_lambda), l, R)
        gcol, grow = _dn_gates(ba, alog, dtb, l, VH)
        factors = _gdn_prep(P, dn_conv_w, gcol, grow, l, (off_q, off_k, off_v, off_z, Q, V))
        y_dn = _gdn_scan(factors, P, gcol, row(dn_norm), l, off_z)
        xm = _merge(xm, y_lru.reshape(B * T, R), y_dn.reshape(B * T, V), P.reshape(B * T, -1),
                    wl, wd, wo, l, off_b)
        xm = _ffn(xm, row(ffn2_norm), ffn2_wi, ffn2_wo, l, 0.5)
    return _final_norm(xm, final_norm[None, :]).reshape(B, T, D)
```

```python
import functools
import math

import numpy as np
import jax
import jax.numpy as jnp
from jax import lax
from jax.experimental import pallas as pl
from jax.experimental.pallas import tpu as pltpu

NORM_EPS = 1e-6
CHUNK = 64
PAIR = 2 * CHUNK
CONV_W = 4
LRU_C = 8.0
LANES = 128
SUBLANES = 8
VMEM_LIMIT = 56 * 2**20

f32 = jnp.float32
bf16 = jnp.bfloat16


def _cparams(*sem):
    return pltpu.CompilerParams(dimension_semantics=sem, vmem_limit_bytes=VMEM_LIMIT)


def _tile(n, pref):
    t = min(n, pref)
    assert n % t == 0, (n, pref)
    return t


def _sigmoid(x):
    return 0.5 * jnp.tanh(0.5 * x) + 0.5


def _softplus(x):
    return jnp.maximum(x, 0.0) + jnp.log1p(jnp.exp(-jnp.abs(x)))


def _rms(x):
    return x * lax.rsqrt(jnp.mean(x * x, axis=-1, keepdims=True) + NORM_EPS)


def _dot(a, b):
    return jnp.dot(a, b, preferred_element_type=f32)


def _dot_nt(a, b):
    return lax.dot_general(a, b, (((1,), (1,)), ((), ())), preferred_element_type=f32)


def _dot_tn(a, b):
    return lax.dot_general(a, b, (((0,), (0,)), ((), ())), preferred_element_type=f32)


def _ffn_kernel(x_ref, g_ref, wg_ref, wu_ref, wo_ref, o_ref, h_ref, *, scale):
    @pl.when(pl.program_id(1) == 0)
    def _():
        x = x_ref[...]
        h_ref[...] = (_rms(x) * g_ref[...]).astype(bf16)
        o_ref[...] = x

    h = h_ref[...]
    gate = _dot(h, wg_ref[...])
    up = _dot(h, wu_ref[...])
    act = (gate * _sigmoid(gate) * up).astype(bf16)
    o_ref[...] += scale * _dot(act, wo_ref[...])


def _ffn(x, g, w_in, w_out, l, scale):
    M, D = x.shape
    F = w_out.shape[1]
    tm, tf = _tile(M, 512), _tile(F, 512)
    nf = F // tf
    return pl.pallas_call(
        functools.partial(_ffn_kernel, scale=scale),
        grid=(M // tm, nf),
        in_specs=[
            pl.BlockSpec((tm, D), lambda i, j: (i, 0)),
            pl.BlockSpec((None, 1, D), lambda i, j: (l, 0, 0)),
            pl.BlockSpec((None, D, tf), lambda i, j: (l, 0, j)),
            pl.BlockSpec((None, D, tf), lambda i, j: (l, 0, j + nf)),
            pl.BlockSpec((None, tf, D), lambda i, j: (l, j, 0)),
        ],
        out_specs=pl.BlockSpec((tm, D), lambda i, j: (i, 0)),
        out_shape=jax.ShapeDtypeStruct((M, D), f32),
        scratch_shapes=[pltpu.VMEM((tm, D), bf16)],
        compiler_params=_cparams("parallel", "arbitrary"),
        name="ffn",
    )(x, g, w_in, w_in, w_out)


def _norm_proj_kernel(x_ref, g_ref, w_ref, o_ref, h_ref):
    @pl.when(pl.program_id(1) == 0)
    def _():
        h_ref[...] = (_rms(x_ref[...]) * g_ref[...]).astype(bf16)

    o_ref[...] = _dot(h_ref[...], w_ref[...])


def _norm_proj(x, g, w, l, name):
    M, D = x.shape
    N = w.shape[2]
    tm, tn = _tile(M, 1024), _tile(N, 512)
    return pl.pallas_call(
        _norm_proj_kernel,
        grid=(M // tm, N // tn),
        in_specs=[
            pl.BlockSpec((tm, D), lambda i, j: (i, 0)),
            pl.BlockSpec((None, 1, D), lambda i, j: (l, 0, 0)),
            pl.BlockSpec((None, D, tn), lambda i, j: (l, 0, j)),
        ],
        out_specs=pl.BlockSpec((tm, tn), lambda i, j: (i, j)),
        out_shape=jax.ShapeDtypeStruct((M, N), f32),
        scratch_shapes=[pltpu.VMEM((tm, D), bf16)],
        compiler_params=_cparams("parallel", "arbitrary"),
        name=name,
    )(x, g, w)


def _lru_kernel(xr_ref, yr_ref, cw_ref, cb_ref, wa_ref, ba_ref, wx_ref, bx_ref, lam_ref,
                o_ref, xe_ref, hc_ref, a_ref, u_ref, *, tt, tc):
    t = pl.program_id(2)

    @pl.when(t == 0)
    def _():
        xe_ref[0:SUBLANES, :] = jnp.zeros((SUBLANES, tc), f32)
        hc_ref[...] = jnp.zeros_like(hc_ref)

    @pl.when(t > 0)
    def _():
        xe_ref[0:SUBLANES, :] = xe_ref[tt:tt + SUBLANES, :]

    x = xr_ref[0]
    xe_ref[SUBLANES:SUBLANES + tt, :] = x
    cw = cw_ref[...]
    xc = cw[3:4] * x + cb_ref[...]
    for s in range(1, CONV_W):
        xc = xc + cw[3 - s:4 - s] * xe_ref[SUBLANES - s:SUBLANES - s + tt, :]

    ra, ri = [], []
    for blk in range(tc // LANES):
        xb = xc[:, blk * LANES:(blk + 1) * LANES].astype(bf16)
        ra.append(_dot(xb, wa_ref[blk]))
        ri.append(_dot(xb, wx_ref[blk]))
    r = _sigmoid(jnp.concatenate(ra, axis=1) + ba_ref[...])
    i = _sigmoid(jnp.concatenate(ri, axis=1) + bx_ref[...])
    log_a = (-LRU_C) * r * _softplus(-lam_ref[...])
    a = jnp.exp(log_a)
    u = xc * i * jnp.sqrt(-jnp.tanh(log_a) * (a * a + 1.0))
    a_ref[...] = a
    u_ref[...] = u

    row = lax.broadcasted_iota(jnp.int32, (SUBLANES, tc), 0)

    def group(g, h):
        r0 = pl.multiple_of(g * SUBLANES, SUBLANES)
        A = a_ref[pl.ds(r0, SUBLANES), :]
        B = u_ref[pl.ds(r0, SUBLANES), :]
        for s in (1, 2, 4):
            keep = row >= s
            B = jnp.where(keep, A * pltpu.roll(B, s, 0) + B, B)
            A = jnp.where(keep, A * pltpu.roll(A, s, 0), A)
        H = A * h + B
        u_ref[pl.ds(r0, SUBLANES), :] = H
        return H[SUBLANES - 1:SUBLANES, :]

    hc_ref[0:1, :] = lax.fori_loop(0, tt // SUBLANES, group, hc_ref[0:1, :], unroll=4)

    y = yr_ref[0]
    gelu = 0.5 * y * (1.0 + jnp.tanh(math.sqrt(2.0 / math.pi) * (y + 0.044715 * (y * y * y))))
    o_ref[0] = (u_ref[...] * gelu).astype(bf16)


def _lru(P, cw, cb, wa, ba, wx, bx, lam, l, R):
    B, T, _ = P.shape
    tt, tc = _tile(T, 512), _tile(R, 512)
    nc = R // tc
    nb = tc // LANES
    vec = lambda: pl.BlockSpec((None, 1, tc), lambda b, c, t: (l, 0, c))
    gw = lambda: pl.BlockSpec((None, nb, LANES, LANES), lambda b, c, t: (l, c, 0, 0))
    return pl.pallas_call(
        functools.partial(_lru_kernel, tt=tt, tc=tc),
        grid=(B, nc, T // tt),
        in_specs=[
            pl.BlockSpec((1, tt, tc), lambda b, c, t: (b, t, c)),
            pl.BlockSpec((1, tt, tc), lambda b, c, t: (b, t, c + nc)),
            pl.BlockSpec((None, CONV_W, tc), lambda b, c, t: (l, 0, c)),
            vec(), gw(), vec(), gw(), vec(), vec(),
        ],
        out_specs=pl.BlockSpec((1, tt, tc), lambda b, c, t: (b, t, c)),
        out_shape=jax.ShapeDtypeStruct((B, T, R), bf16),
        scratch_shapes=[
            pltpu.VMEM((tt + SUBLANES, tc), f32),
            pltpu.VMEM((SUBLANES, tc), f32),
            pltpu.VMEM((tt, tc), f32),
            pltpu.VMEM((tt, tc), f32),
        ],
        compiler_params=_cparams("parallel", "parallel", "arbitrary"),
        name="lru",
    )(P, P, cw, cb, wa, ba, wx, bx, lam)


def _dn_gate_kernel(ba_ref, alog_ref, dtb_ref, tril_ref, pcol_ref, prow_ref, gcol_ref, grow_ref, *, vh):
    ba = ba_ref[0]
    lane = lax.broadcasted_iota(jnp.int32, ba.shape, 1)
    beta = _sigmoid(ba)
    g = -jnp.exp(alog_ref[...]) * _softplus(ba + dtb_ref[...])
    gcum = jnp.dot(tril_ref[...], g, preferred_element_type=f32, precision=lax.Precision.HIGHEST)
    vals = jnp.where(lane < vh, beta, gcum)
    gcol_ref[0] = jnp.dot(vals, pcol_ref[...], preferred_element_type=f32, precision=lax.Precision.HIGHEST)
    grow_ref[0] = lax.dot_general(prow_ref[...], vals, (((1,), (1,)), ((), ())),
                                  preferred_element_type=f32, precision=lax.Precision.HIGHEST)


def _dn_gate_consts(vh, tt):
    kh = vh // 2
    pcol = np.zeros((LANES, kh * LANES), np.float32)
    prow = np.zeros((kh * SUBLANES, LANES), np.float32)
    for p in range(kh):
        for m, src in enumerate((2 * p, 2 * p + 1, vh + 2 * p, vh + 2 * p + 1)):
            pcol[src, p * LANES + m] = 1.0
            prow[p * SUBLANES + m, src] = 1.0
    idx = np.arange(tt)
    tril = ((idx[:, None] // CHUNK == idx[None, :] // CHUNK) & (idx[None, :] <= idx[:, None])).astype(np.float32)
    return jnp.asarray(tril), jnp.asarray(pcol), jnp.asarray(prow)


def _dn_gates(ba, alog, dtb, l, vh):
    B, T, _ = ba.shape
    kh = vh // 2
    tt = _tile(T, 256)
    tril, pcol, prow = _dn_gate_consts(vh, tt)
    full = lambda a: pl.BlockSpec(a.shape, lambda b, t: (0,) * a.ndim)
    return pl.pallas_call(
        functools.partial(_dn_gate_kernel, vh=vh),
        grid=(B, T // tt),
        in_specs=[
            pl.BlockSpec((1, tt, LANES), lambda b, t: (b, t, 0)),
            pl.BlockSpec((None, 1, LANES), lambda b, t: (l, 0, 0)),
            pl.BlockSpec((None, 1, LANES), lambda b, t: (l, 0, 0)),
            full(tril), full(pcol), full(prow),
        ],
        out_specs=[
            pl.BlockSpec((1, tt, kh * LANES), lambda b, t: (b, t, 0)),
            pl.BlockSpec((1, kh * SUBLANES, tt), lambda b, t: (b, 0, t)),
        ],
        out_shape=[
            jax.ShapeDtypeStruct((B, T, kh * LANES), f32),
            jax.ShapeDtypeStruct((B, kh * SUBLANES, T), f32),
        ],
        compiler_params=_cparams("parallel", "parallel"),
        name="dn_gates",
    )(ba, alog, dtb, tril, pcol, prow)


def _conv_silu(x_ref, halo_ref, w_ref, xe_ref, t, tt):
    xe_ref[0:SUBLANES, :] = jnp.where(t > 0, halo_ref[0], 0.0)
    x = x_ref[0]
    xe_ref[SUBLANES:SUBLANES + tt, :] = x
    w = w_ref[...]
    y = w[3:4] * x
    for s in range(1, CONV_W):
        y = y + w[3 - s:4 - s] * xe_ref[SUBLANES - s:SUBLANES - s + tt, :]
    return y * _sigmoid(y)


def _gdn_prep_kernel(q_ref, k_ref, v_ref, qh_ref, kh_ref, vh_ref, wq_ref, wk_ref, wv_ref, gcol_ref, grow_ref,
                     u_ref, w_ref, qd_ref, kd_ref, aqk_ref, qe_ref, ke_ref, ve_ref, *, tt):
    t = pl.program_id(2)
    q = _conv_silu(q_ref, qh_ref, wq_ref, qe_ref, t, tt)
    k = _conv_silu(k_ref, kh_ref, wk_ref, ke_ref, t, tt)
    v = _conv_silu(v_ref, vh_ref, wv_ref, ve_ref, t, tt)
    q = q * lax.rsqrt(jnp.sum(q * q, axis=-1, keepdims=True) + NORM_EPS) * (LANES ** -0.5)
    k = k * lax.rsqrt(jnp.sum(k * k, axis=-1, keepdims=True) + NORM_EPS)

    ri = lax.broadcasted_iota(jnp.int32, (PAIR, PAIR), 0)
    ci = lax.broadcasted_iota(jnp.int32, (PAIR, PAIR), 1)
    same_chunk = (ri >= CHUNK) == (ci >= CHUNK)
    causal = same_chunk & (ci <= ri)
    strict = same_chunk & (ci < ri)
    eye = (ri == ci).astype(f32)
    first = lax.broadcasted_iota(jnp.int32, (PAIR, 1), 0) < CHUNK
    levels = int(math.log2(CHUNK)) - 2

    pairs = range(tt // PAIR)
    rows = [slice(p * PAIR, (p + 1) * PAIR) for p in pairs]
    kb = [k[r].astype(bf16) for r in rows]
    kk = [_dot_nt(kb[p], kb[p]) for p in pairs]
    qk = [_dot_nt(q[rows[p]].astype(bf16), kb[p]) for p in pairs]

    chains = [(p, j) for p in pairs for j in range(2)]
    beta = [gcol_ref[0, rows[p], j:j + 1] for p, j in chains]
    gc = [gcol_ref[0, rows[p], 2 + j:3 + j] for p, j in chains]
    decay = [jnp.where(causal, jnp.exp(gc[i] - grow_ref[0, 2 + j:3 + j, rows[p]]), 0.0)
             for i, (p, j) in enumerate(chains)]
    N = [jnp.where(strict, -(beta[i] * kk[p] * decay[i]), 0.0) for i, (p, j) in enumerate(chains)]
    X = [eye + n for n in N]
    Nb = [n.astype(bf16) for n in N]
    P = [_dot(nb, nb) for nb in Nb]
    for _ in range(levels):
        Pb = [m.astype(bf16) for m in P]
        PX = [_dot(Pb[i], jnp.concatenate([Pb[i], X[i].astype(bf16)], axis=1)) for i in range(len(chains))]
        P = [m[:, :PAIR] for m in PX]
        X = [X[i] + PX[i][:, PAIR:] for i in range(len(chains))]
    X = [X[i] + _dot(P[i].astype(bf16), X[i].astype(bf16)) for i in range(len(chains))]

    for i, (p, j) in enumerate(chains):
        qp, kp = q[rows[p]], k[rows[p]]
        eg = jnp.exp(gc[i])
        vp = v[rows[p], j * LANES:(j + 1) * LANES]
        rhs = jnp.concatenate([vp * beta[i], kp * (beta[i] * eg)], axis=1).astype(bf16)
        sol = _dot(X[i].astype(bf16), rhs)
        g_last = jnp.where(first, gc[i][CHUNK - 1:CHUNK, :], gc[i][PAIR - 1:PAIR, :])
        u_ref[0, j, rows[p], :] = sol[:, :LANES]
        w_ref[0, j, rows[p], :] = sol[:, LANES:].astype(bf16)
        qd_ref[0, j, rows[p], :] = (qp * eg).astype(bf16)
        kd_ref[0, j, rows[p], :] = (kp * jnp.exp(g_last - gc[i])).astype(bf16)
        aqk_ref[0, j, rows[p], :] = (qk[p] * decay[i]).astype(bf16)


def _gdn_prep(P, cw, gcol, grow, l, offs):
    B, T, _ = P.shape
    off_q, off_k, off_v, off_z, Q, V = offs
    KH = Q // LANES
    VH = 2 * KH
    tt = _tile(T, 512)
    hw = 2 * LANES
    cq, ck, cv = off_q // LANES, off_k // LANES, off_v // hw
    wk0, wv0 = Q // LANES, (2 * Q) // hw
    hb = tt // SUBLANES
    prev = lambda t: jnp.maximum(t * hb - 1, 0)
    head_out = lambda dt: (pl.BlockSpec((1, 2, tt, LANES), lambda b, h, t: (b, h, t, 0)),
                           jax.ShapeDtypeStruct((B, VH, T, LANES), dt))
    outs = [head_out(f32)] + [head_out(bf16) for _ in range(4)]
    return pl.pallas_call(
        functools.partial(_gdn_prep_kernel, tt=tt),
        grid=(B, KH, T // tt),
        in_specs=[
            pl.BlockSpec((1, tt, LANES), lambda b, h, t: (b, t, cq + h)),
            pl.BlockSpec((1, tt, LANES), lambda b, h, t: (b, t, ck + h)),
            pl.BlockSpec((1, tt, hw), lambda b, h, t: (b, t, cv + h)),
            pl.BlockSpec((1, SUBLANES, LANES), lambda b, h, t: (b, prev(t), cq + h)),
            pl.BlockSpec((1, SUBLANES, LANES), lambda b, h, t: (b, prev(t), ck + h)),
            pl.BlockSpec((1, SUBLANES, hw), lambda b, h, t: (b, prev(t), cv + h)),
            pl.BlockSpec((None, CONV_W, LANES), lambda b, h, t: (l, 0, h)),
            pl.BlockSpec((None, CONV_W, LANES), lambda b, h, t: (l, 0, wk0 + h)),
            pl.BlockSpec((None, CONV_W, hw), lambda b, h, t: (l, 0, wv0 + h)),
            pl.BlockSpec((1, tt, LANES), lambda b, h, t: (b, t, h)),
            pl.BlockSpec((1, SUBLANES, tt), lambda b, h, t: (b, h, t)),
        ],
        out_specs=[o[0] for o in outs],
        out_shape=[o[1] for o in outs],
        scratch_shapes=[
            pltpu.VMEM((tt + SUBLANES, LANES), f32),
            pltpu.VMEM((tt + SUBLANES, LANES), f32),
            pltpu.VMEM((tt + SUBLANES, hw), f32),
        ],
        compiler_params=_cparams("parallel", "parallel", "parallel"),
        name="gdn_prep",
    )(P, P, P, P, P, P, cw, cw, cw, gcol, grow)


def _gdn_scan_kernel(u_ref, w_ref, qd_ref, kd_ref, aqk_ref, z_ref, gcol_ref, nrm_ref, o_ref, s_ref, *, tt, nh):
    @pl.when(pl.program_id(2) == 0)
    def _():
        s_ref[...] = jnp.zeros_like(s_ref)

    nrm = nrm_ref[...]
    zeros = jnp.zeros((CHUNK, LANES), bf16)
    heads = range(nh)
    S = [s_ref[g] for g in heads]
    v_prev = [zeros for _ in heads]
    for c in range(tt // CHUNK):
        rows = slice(c * CHUNK, (c + 1) * CHUNK)
        ws_qs = [_dot(jnp.concatenate([w_ref[0, g, rows, :], qd_ref[0, g, rows, :]], axis=0),
                      S[g].astype(bf16)) for g in heads]
        v_new = [(u_ref[0, g, rows, :] - ws_qs[g][:CHUNK]).astype(bf16) for g in heads]
        v_pair = [jnp.concatenate([v_new[g], zeros] if c % 2 == 0 else [v_prev[g], v_new[g]], axis=0)
                  for g in heads]
        o = [ws_qs[g][CHUNK:] + _dot(aqk_ref[0, g, rows, :], v_pair[g]) for g in heads]
        for g in heads:
            lane_g = (g // 2) * LANES + 2 + g % 2
            g_last = gcol_ref[0, (c + 1) * CHUNK - 1:(c + 1) * CHUNK, lane_g:lane_g + 1]
            S[g] = S[g] * jnp.exp(g_last) + _dot_tn(kd_ref[0, g, rows, :], v_new[g])
        for g in heads:
            z = z_ref[0, rows, g * LANES:(g + 1) * LANES]
            o_ref[0, rows, g * LANES:(g + 1) * LANES] = (_rms(o[g]) * nrm * (z * _sigmoid(z))).astype(bf16)
        v_prev = v_new
    for g in heads:
        s_ref[g] = S[g]


def _gdn_scan(factors, P, gcol, nrm, l, off_z):
    u = factors[0]
    B, VH, T, _ = u.shape
    nh = _tile(VH, 8)
    tt = _tile(T, 256)
    cz = off_z // (nh * LANES)
    assert off_z % (nh * LANES) == 0 and nh % 2 == 0
    head_in = lambda: pl.BlockSpec((1, nh, tt, LANES), lambda b, h, t: (b, h, t, 0))
    return pl.pallas_call(
        functools.partial(_gdn_scan_kernel, tt=tt, nh=nh),
        grid=(B, VH // nh, T // tt),
        in_specs=[
            head_in(), head_in(), head_in(), head_in(), head_in(),
            pl.BlockSpec((1, tt, nh * LANES), lambda b, h, t: (b, t, cz + h)),
            pl.BlockSpec((1, tt, (nh // 2) * LANES), lambda b, h, t: (b, t, h)),
            pl.BlockSpec((None, 1, LANES), lambda b, h, t: (l, 0, 0)),
        ],
        out_specs=pl.BlockSpec((1, tt, nh * LANES), lambda b, h, t: (b, t, h)),
        out_shape=jax.ShapeDtypeStruct((B, T, VH * LANES), bf16),
        scratch_shapes=[pltpu.VMEM((nh, LANES, LANES), f32)],
        compiler_params=_cparams("parallel", "parallel", "arbitrary"),
        name="gdn_scan",
    )(*factors, P, gcol, nrm)


def _merge_kernel(x_ref, yl_ref, yd_ref, gl_ref, gd_ref, wl_ref, wd_ref, wo_ref, o_ref):
    @pl.when(pl.program_id(1) == 0)
    def _():
        o_ref[...] = x_ref[...]

    a = _dot(yl_ref[...], wl_ref[...])
    b = _dot(yd_ref[...], wd_ref[...])
    merged = (_sigmoid(gl_ref[...]) * a + _sigmoid(gd_ref[...]) * b).astype(bf16)
    o_ref[...] += _dot(merged, wo_ref[...])


def _merge(x, yl, yd, G, wl, wd, wo, l):
    M, D = x.shape
    R, V = yl.shape[1], yd.shape[1]
    tm, tn = _tile(M, 512), _tile(D, 512)
    nd = D // tn
    return pl.pallas_call(
        _merge_kernel,
        grid=(M // tm, nd),
        in_specs=[
            pl.BlockSpec((tm, D), lambda i, j: (i, 0)),
            pl.BlockSpec((tm, R), lambda i, j: (i, 0)),
            pl.BlockSpec((tm, V), lambda i, j: (i, 0)),
            pl.BlockSpec((tm, tn), lambda i, j: (i, j)),
            pl.BlockSpec((tm, tn), lambda i, j: (i, nd + j)),
            pl.BlockSpec((None, R, tn), lambda i, j: (l, 0, j)),
            pl.BlockSpec((None, V, tn), lambda i, j: (l, 0, j)),
            pl.BlockSpec((None, tn, D), lambda i, j: (l, j, 0)),
        ],
        out_specs=pl.BlockSpec((tm, D), lambda i, j: (i, 0)),
        out_shape=jax.ShapeDtypeStruct((M, D), f32),
        compiler_params=_cparams("parallel", "arbitrary"),
        name="merge",
    )(x, yl, yd, G, G, wl, wd, wo)


def _final_norm_kernel(x_ref, g_ref, o_ref):
    o_ref[...] = _rms(x_ref[...]) * g_ref[...]


def _final_norm(x, g):
    M, D = x.shape
    tm = _tile(M, 512)
    return pl.pallas_call(
        _final_norm_kernel,
        grid=(M // tm,),
        in_specs=[pl.BlockSpec((tm, D), lambda i: (i, 0)), pl.BlockSpec((1, D), lambda i: (0, 0))],
        out_specs=pl.BlockSpec((tm, D), lambda i: (i, 0)),
        out_shape=jax.ShapeDtypeStruct((M, D), f32),
        compiler_params=_cparams("parallel"),
        name="final_norm",
    )(x, g)


def kernel(x, ffn1_norm, ffn1_w_in, ffn1_w_out, mix_norm, w_in, lru_conv_w, lru_conv_b, lru_gate_a_w, lru_gate_a_b, lru_gate_x_w, lru_gate_x_b, lru_lambda, dn_conv_w, dn_A_log, dn_dt_bias, dn_norm, w_branch_lru, w_branch_dn, w_out, ffn2_norm, ffn2_w_in, ffn2_w_out, final_norm):
    B, T, D = x.shape
    depth = ffn1_norm.shape[0]
    R = lru_conv_w.shape[-1]
    VH = dn_A_log.shape[-1]
    V = VH * LANES
    Q = (dn_conv_w.shape[-1] - V) // 2
    assert Q // LANES == VH // 2 and 2 * VH <= LANES and T % PAIR == 0
    off_q = 2 * R
    off_k = off_q + Q
    off_v = off_k + Q
    off_z = off_v + V
    off_b = off_z + V
    off_gates = off_b + 2 * VH
    assert w_in.shape[-1] == off_gates + 2 * D

    row = lambda p: p[:, None, :]
    w_main = w_in[:, :, :off_b].astype(bf16)
    w_ba = jnp.pad(w_in[:, :, off_b:off_gates], ((0, 0), (0, 0), (0, LANES - 2 * VH))).astype(bf16)
    w_gates = w_in[:, :, off_gates:].astype(bf16)
    ffn1_wi, ffn1_wo = ffn1_w_in.astype(bf16), ffn1_w_out.astype(bf16)
    ffn2_wi, ffn2_wo = ffn2_w_in.astype(bf16), ffn2_w_out.astype(bf16)
    wl, wd, wo = w_branch_lru.astype(bf16), w_branch_dn.astype(bf16), w_out.astype(bf16)
    wa, wx = lru_gate_a_w.astype(bf16), lru_gate_x_w.astype(bf16)
    pad_a = lambda p: jnp.pad(p, ((0, 0), (VH, LANES - 2 * VH)))[:, None, :]
    alog, dtb = pad_a(dn_A_log), pad_a(dn_dt_bias)

    xm = x.reshape(B * T, D)
    for l in range(depth):
        xm = _ffn(xm, row(ffn1_norm), ffn1_wi, ffn1_wo, l, 0.5)
        P = _norm_proj(xm, row(mix_norm), w_main, l, "in_proj").reshape(B, T, -1)
        ba = _norm_proj(xm, row(mix_norm), w_ba, l, "ba_proj").reshape(B, T, LANES)
        gates = _norm_proj(xm, row(mix_norm), w_gates, l, "gate_proj")
        y_lru = _lru(P, lru_conv_w, row(lru_conv_b), wa, row(lru_gate_a_b), wx, row(lru_gate_x_b),
                     row(lru_lambda), l, R)
        gcol, grow = _dn_gates(ba, alog, dtb, l, VH)
        factors = _gdn_prep(P, dn_conv_w, gcol, grow, l, (off_q, off_k, off_v, off_z, Q, V))
        y_dn = _gdn_scan(factors, P, gcol, row(dn_norm), l, off_z)
        xm = _merge(xm, y_lru.reshape(B * T, R), y_dn.reshape(B * T, V), gates, wl, wd, wo, l)
        xm = _ffn(xm, row(ffn2_norm), ffn2_wi, ffn2_wo, l, 0.5)
    return _final_norm(xm, final_norm[None, :]).reshape(B, T, D)
```

```python
import functools
import math

import numpy as np
import jax
import jax.numpy as jnp
from jax import lax
from jax.experimental import pallas as pl
from jax.experimental.pallas import tpu as pltpu

NORM_EPS = 1e-6
CHUNK = 64
PAIR = 2 * CHUNK
CONV_W = 4
LRU_C = 8.0
LANES = 128
SUBLANES = 8
VMEM_LIMIT = 56 * 2**20

f32 = jnp.float32
bf16 = jnp.bfloat16


def _cparams(*sem):
    return pltpu.CompilerParams(dimension_semantics=sem, vmem_limit_bytes=VMEM_LIMIT)


def _tile(n, pref):
    t = min(n, pref)
    assert n % t == 0, (n, pref)
    return t


def _sigmoid(x):
    return 0.5 * jnp.tanh(0.5 * x) + 0.5


def _softplus(x):
    return jnp.maximum(x, 0.0) + jnp.log1p(jnp.exp(-jnp.abs(x)))


def _rms(x):
    return x * lax.rsqrt(jnp.mean(x * x, axis=-1, keepdims=True) + NORM_EPS)


def _dot(a, b):
    return jnp.dot(a, b, preferred_element_type=f32)


def _dot_nt(a, b):
    return lax.dot_general(a, b, (((1,), (1,)), ((), ())), preferred_element_type=f32)


def _dot_tn(a, b):
    return lax.dot_general(a, b, (((0,), (0,)), ((), ())), preferred_element_type=f32)


def _ffn_kernel(x_ref, g_ref, wg_ref, wu_ref, wo_ref, o_ref, h_ref, *, scale):
    @pl.when(pl.program_id(1) == 0)
    def _():
        x = x_ref[...]
        h_ref[...] = (_rms(x) * g_ref[...]).astype(bf16)
        o_ref[...] = x

    h = h_ref[...]
    gate = _dot(h, wg_ref[...])
    up = _dot(h, wu_ref[...])
    act = (gate * _sigmoid(gate) * up).astype(bf16)
    o_ref[...] += scale * _dot(act, wo_ref[...])


def _ffn(x, g, w_in, w_out, l, scale):
    M, D = x.shape
    F = w_out.shape[1]
    tm, tf = _tile(M, 1024), _tile(F, 512)
    nf = F // tf
    return pl.pallas_call(
        functools.partial(_ffn_kernel, scale=scale),
        grid=(M // tm, nf),
        in_specs=[
            pl.BlockSpec((tm, D), lambda i, j: (i, 0)),
            pl.BlockSpec((None, 1, D), lambda i, j: (l, 0, 0)),
            pl.BlockSpec((None, D, tf), lambda i, j: (l, 0, j)),
            pl.BlockSpec((None, D, tf), lambda i, j: (l, 0, j + nf)),
            pl.BlockSpec((None, tf, D), lambda i, j: (l, j, 0)),
        ],
        out_specs=pl.BlockSpec((tm, D), lambda i, j: (i, 0)),
        out_shape=jax.ShapeDtypeStruct((M, D), f32),
        scratch_shapes=[pltpu.VMEM((tm, D), bf16)],
        compiler_params=_cparams("parallel", "arbitrary"),
        name="ffn",
    )(x, g, w_in, w_in, w_out)


def _norm_proj_kernel(x_ref, g_ref, w_ref, o_ref, h_ref):
    @pl.when(pl.program_id(1) == 0)
    def _():
        h_ref[...] = (_rms(x_ref[...]) * g_ref[...]).astype(bf16)

    o_ref[...] = _dot(h_ref[...], w_ref[...])


def _norm_proj(x, g, w, l, name, n_cols=None):
    M, D = x.shape
    N = w.shape[2] if n_cols is None else n_cols
    tm, tn = _tile(M, 1024), _tile(N, 1024)
    return pl.pallas_call(
        _norm_proj_kernel,
        grid=(M // tm, N // tn),
        in_specs=[
            pl.BlockSpec((tm, D), lambda i, j: (i, 0)),
            pl.BlockSpec((None, 1, D), lambda i, j: (l, 0, 0)),
            pl.BlockSpec((None, D, tn), lambda i, j: (l, 0, j)),
        ],
        out_specs=pl.BlockSpec((tm, tn), lambda i, j: (i, j)),
        out_shape=jax.ShapeDtypeStruct((M, N), f32),
        scratch_shapes=[pltpu.VMEM((tm, D), bf16)],
        compiler_params=_cparams("parallel", "arbitrary"),
        name=name,
    )(x, g, w)


def _lru_kernel(xr_ref, yr_ref, cw_ref, cb_ref, wa_ref, ba_ref, wx_ref, bx_ref, lam_ref,
                o_ref, xe_ref, hc_ref, a_ref, u_ref, *, tt, tc):
    t = pl.program_id(2)

    @pl.when(t == 0)
    def _():
        xe_ref[0:SUBLANES, :] = jnp.zeros((SUBLANES, tc), f32)
        hc_ref[...] = jnp.zeros_like(hc_ref)

    @pl.when(t > 0)
    def _():
        xe_ref[0:SUBLANES, :] = xe_ref[tt:tt + SUBLANES, :]

    x = xr_ref[0]
    xe_ref[SUBLANES:SUBLANES + tt, :] = x
    cw = cw_ref[...]
    xc = cw[3:4] * x + cb_ref[...]
    for s in range(1, CONV_W):
        xc = xc + cw[3 - s:4 - s] * xe_ref[SUBLANES - s:SUBLANES - s + tt, :]

    ra, ri = [], []
    for blk in range(tc // LANES):
        xb = xc[:, blk * LANES:(blk + 1) * LANES].astype(bf16)
        ra.append(_dot(xb, wa_ref[blk]))
        ri.append(_dot(xb, wx_ref[blk]))
    r = _sigmoid(jnp.concatenate(ra, axis=1) + ba_ref[...])
    i = _sigmoid(jnp.concatenate(ri, axis=1) + bx_ref[...])
    log_a = (-LRU_C) * r * _softplus(-lam_ref[...])
    a = jnp.exp(log_a)
    u = xc * i * jnp.sqrt(-jnp.tanh(log_a) * (a * a + 1.0))
    a_ref[...] = a
    u_ref[...] = u

    row = lax.broadcasted_iota(jnp.int32, (SUBLANES, tc), 0)

    def group(g, h):
        r0 = pl.multiple_of(g * SUBLANES, SUBLANES)
        A = a_ref[pl.ds(r0, SUBLANES), :]
        B = u_ref[pl.ds(r0, SUBLANES), :]
        for s in (1, 2, 4):
            keep = row >= s
            B = jnp.where(keep, A * pltpu.roll(B, s, 0) + B, B)
            A = jnp.where(keep, A * pltpu.roll(A, s, 0), A)
        H = A * h + B
        u_ref[pl.ds(r0, SUBLANES), :] = H
        return H[SUBLANES - 1:SUBLANES, :]

    hc_ref[0:1, :] = lax.fori_loop(0, tt // SUBLANES, group, hc_ref[0:1, :], unroll=4)

    y = yr_ref[0]
    gelu = 0.5 * y * (1.0 + jnp.tanh(math.sqrt(2.0 / math.pi) * (y + 0.044715 * (y * y * y))))
    o_ref[0] = (u_ref[...] * gelu).astype(bf16)


def _lru(P, cw, cb, wa, ba, wx, bx, lam, l, R):
    B, T, _ = P.shape
    tt, tc = _tile(T, 512), _tile(R, 512)
    nc = R // tc
    nb = tc // LANES
    vec = lambda: pl.BlockSpec((None, 1, tc), lambda b, c, t: (l, 0, c))
    gw = lambda: pl.BlockSpec((None, nb, LANES, LANES), lambda b, c, t: (l, c, 0, 0))
    return pl.pallas_call(
        functools.partial(_lru_kernel, tt=tt, tc=tc),
        grid=(B, nc, T // tt),
        in_specs=[
            pl.BlockSpec((1, tt, tc), lambda b, c, t: (b, t, c)),
            pl.BlockSpec((1, tt, tc), lambda b, c, t: (b, t, c + nc)),
            pl.BlockSpec((None, CONV_W, tc), lambda b, c, t: (l, 0, c)),
            vec(), gw(), vec(), gw(), vec(), vec(),
        ],
        out_specs=pl.BlockSpec((1, tt, tc), lambda b, c, t: (b, t, c)),
        out_shape=jax.ShapeDtypeStruct((B, T, R), bf16),
        scratch_shapes=[
            pltpu.VMEM((tt + SUBLANES, tc), f32),
            pltpu.VMEM((SUBLANES, tc), f32),
            pltpu.VMEM((tt, tc), f32),
            pltpu.VMEM((tt, tc), f32),
        ],
        compiler_params=_cparams("parallel", "parallel", "arbitrary"),
        name="lru",
    )(P, P, cw, cb, wa, ba, wx, bx, lam)


def _dn_gate_kernel(ba_ref, alog_ref, dtb_ref, tril_ref, prow_ref, gcol_ref, grow_ref, *, vh):
    ba = ba_ref[0]
    lane = lax.broadcasted_iota(jnp.int32, ba.shape, 1)
    beta = _sigmoid(ba)
    g = -jnp.exp(alog_ref[...]) * _softplus(ba + dtb_ref[...])
    gcum = jnp.dot(tril_ref[...], g, preferred_element_type=f32, precision=lax.Precision.HIGHEST)
    vals = jnp.where(lane < vh, beta, gcum)
    gcol_ref[0] = vals
    grow_ref[0] = lax.dot_general(prow_ref[...], vals, (((1,), (1,)), ((), ())),
                                  preferred_element_type=f32, precision=lax.Precision.HIGHEST)


def _dn_gate_consts(vh, tt):
    kh = vh // 2
    prow = np.zeros((kh * SUBLANES, LANES), np.float32)
    for p in range(kh):
        for m, src in enumerate((2 * p, 2 * p + 1, vh + 2 * p, vh + 2 * p + 1)):
            prow[p * SUBLANES + m, src] = 1.0
    idx = np.arange(tt)
    tril = ((idx[:, None] // CHUNK == idx[None, :] // CHUNK) & (idx[None, :] <= idx[:, None])).astype(np.float32)
    return jnp.asarray(tril), jnp.asarray(prow)


def _dn_gates(ba, alog, dtb, l, vh):
    B, T, _ = ba.shape
    kh = vh // 2
    tt = _tile(T, 256)
    tril, prow = _dn_gate_consts(vh, tt)
    full = lambda a: pl.BlockSpec(a.shape, lambda b, t: (0,) * a.ndim)
    return pl.pallas_call(
        functools.partial(_dn_gate_kernel, vh=vh),
        grid=(B, T // tt),
        in_specs=[
            pl.BlockSpec((1, tt, LANES), lambda b, t: (b, t, 0)),
            pl.BlockSpec((None, 1, LANES), lambda b, t: (l, 0, 0)),
            pl.BlockSpec((None, 1, LANES), lambda b, t: (l, 0, 0)),
            full(tril), full(prow),
        ],
        out_specs=[
            pl.BlockSpec((1, tt, LANES), lambda b, t: (b, t, 0)),
            pl.BlockSpec((1, kh * SUBLANES, tt), lambda b, t: (b, 0, t)),
        ],
        out_shape=[
            jax.ShapeDtypeStruct((B, T, LANES), f32),
            jax.ShapeDtypeStruct((B, kh * SUBLANES, T), f32),
        ],
        compiler_params=_cparams("parallel", "parallel"),
        name="dn_gates",
    )(ba, alog, dtb, tril, prow)


def _conv_silu(x_ref, halo_ref, w_ref, xe_ref, t, tt):
    xe_ref[0:SUBLANES, :] = jnp.where(t > 0, halo_ref[0], 0.0)
    x = x_ref[0]
    xe_ref[SUBLANES:SUBLANES + tt, :] = x
    w = w_ref[...]
    y = w[3:4] * x
    for s in range(1, CONV_W):
        y = y + w[3 - s:4 - s] * xe_ref[SUBLANES - s:SUBLANES - s + tt, :]
    return y * _sigmoid(y)


def _gdn_prep_kernel(q_ref, k_ref, v_ref, qh_ref, kh_ref, vh_ref, wq_ref, wk_ref, wv_ref, gcol_ref, grow_ref,
                     u_ref, w_ref, qd_ref, kd_ref, aqk_ref, egl_ref, qe_ref, ke_ref, ve_ref, *, tt, vh):
    t = pl.program_id(2)
    khead = pl.program_id(1)
    q = _conv_silu(q_ref, qh_ref, wq_ref, qe_ref, t, tt)
    k = _conv_silu(k_ref, kh_ref, wk_ref, ke_ref, t, tt)
    v = _conv_silu(v_ref, vh_ref, wv_ref, ve_ref, t, tt)
    q = q * lax.rsqrt(jnp.sum(q * q, axis=-1, keepdims=True) + NORM_EPS) * (LANES ** -0.5)
    k = k * lax.rsqrt(jnp.sum(k * k, axis=-1, keepdims=True) + NORM_EPS)

    ri = lax.broadcasted_iota(jnp.int32, (PAIR, PAIR), 0)
    ci = lax.broadcasted_iota(jnp.int32, (PAIR, PAIR), 1)
    same_chunk = (ri >= CHUNK) == (ci >= CHUNK)
    causal = same_chunk & (ci <= ri)
    strict = same_chunk & (ci < ri)
    eye = (ri == ci).astype(f32)
    first = lax.broadcasted_iota(jnp.int32, (PAIR, 1), 0) < CHUNK
    levels = int(math.log2(CHUNK)) - 2

    pairs = range(tt // PAIR)
    rows = [slice(p * PAIR, (p + 1) * PAIR) for p in pairs]
    kb = [k[r].astype(bf16) for r in rows]
    kk = [_dot_nt(kb[p], kb[p]) for p in pairs]
    qk = [_dot_nt(q[rows[p]].astype(bf16), kb[p]) for p in pairs]

    chains = [(p, j) for p in pairs for j in range(2)]
    lane = lax.broadcasted_iota(jnp.int32, (1, LANES), 1)

    def column(r, idx):
        return jnp.sum(jnp.where(lane == idx, gcol_ref[0, r, :], 0.0), axis=-1, keepdims=True)

    beta = [column(rows[p], 2 * khead + j) for p, j in chains]
    gc = [column(rows[p], vh + 2 * khead + j) for p, j in chains]
    decay = [jnp.where(causal, jnp.exp(gc[i] - grow_ref[0, 2 + j:3 + j, rows[p]]), 0.0)
             for i, (p, j) in enumerate(chains)]
    N = [jnp.where(strict, -(beta[i] * kk[p] * decay[i]), 0.0) for i, (p, j) in enumerate(chains)]
    X = [eye + n for n in N]
    Nb = [n.astype(bf16) for n in N]
    P = [_dot(nb, nb) for nb in Nb]
    for _ in range(levels):
        Pb = [m.astype(bf16) for m in P]
        PX = [_dot(Pb[i], jnp.concatenate([Pb[i], X[i].astype(bf16)], axis=1)) for i in range(len(chains))]
        P = [m[:, :PAIR] for m in PX]
        X = [X[i] + PX[i][:, PAIR:] for i in range(len(chains))]
    X = [X[i] + _dot(P[i].astype(bf16), X[i].astype(bf16)) for i in range(len(chains))]

    for i, (p, j) in enumerate(chains):
        qp, kp = q[rows[p]], k[rows[p]]
        eg = jnp.exp(gc[i])
        vp = v[rows[p], j * LANES:(j + 1) * LANES]
        rhs = jnp.concatenate([vp * beta[i], kp * (beta[i] * eg)], axis=1).astype(bf16)
        sol = _dot(X[i].astype(bf16), rhs)
        g_last = jnp.where(first, gc[i][CHUNK - 1:CHUNK, :], gc[i][PAIR - 1:PAIR, :])
        u_ref[0, j, rows[p], :] = sol[:, :LANES]
        w_ref[0, j, rows[p], :] = sol[:, LANES:].astype(bf16)
        qd_ref[0, j, rows[p], :] = (qp * eg).astype(bf16)
        kd_ref[0, j, rows[p], :] = (kp * jnp.exp(g_last - gc[i])).astype(bf16)
        aqk_ref[0, j, rows[p], :] = (qk[p] * decay[i]).astype(bf16)
        for c, r in ((2 * p, CHUNK - 1), (2 * p + 1, PAIR - 1)):
            egl_ref[0, j, c:c + 1, :] = jnp.broadcast_to(jnp.exp(gc[i][r:r + 1, :]), (1, LANES))


def _gdn_prep(P, cw, gcol, grow, l, offs):
    B, T, _ = P.shape
    off_q, off_k, off_v, off_z, Q, V = offs
    KH = Q // LANES
    VH = 2 * KH
    tt = _tile(T, 512)
    hw = 2 * LANES
    cq, ck, cv = off_q // LANES, off_k // LANES, off_v // hw
    wk0, wv0 = Q // LANES, (2 * Q) // hw
    hb = tt // SUBLANES
    prev = lambda t: jnp.maximum(t * hb - 1, 0)
    head_out = lambda dt: (pl.BlockSpec((1, 2, tt, LANES), lambda b, h, t: (b, h, t, 0)),
                           jax.ShapeDtypeStruct((B, VH, T, LANES), dt))
    outs = [head_out(f32)] + [head_out(bf16) for _ in range(4)]
    outs.append((pl.BlockSpec((1, 2, tt // CHUNK, LANES), lambda b, h, t: (b, h, t, 0)),
                 jax.ShapeDtypeStruct((B, VH, T // CHUNK, LANES), f32)))
    return pl.pallas_call(
        functools.partial(_gdn_prep_kernel, tt=tt, vh=VH),
        grid=(B, KH, T // tt),
        in_specs=[
            pl.BlockSpec((1, tt, LANES), lambda b, h, t: (b, t, cq + h)),
            pl.BlockSpec((1, tt, LANES), lambda b, h, t: (b, t, ck + h)),
            pl.BlockSpec((1, tt, hw), lambda b, h, t: (b, t, cv + h)),
            pl.BlockSpec((1, SUBLANES, LANES), lambda b, h, t: (b, prev(t), cq + h)),
            pl.BlockSpec((1, SUBLANES, LANES), lambda b, h, t: (b, prev(t), ck + h)),
            pl.BlockSpec((1, SUBLANES, hw), lambda b, h, t: (b, prev(t), cv + h)),
            pl.BlockSpec((None, CONV_W, LANES), lambda b, h, t: (l, 0, h)),
            pl.BlockSpec((None, CONV_W, LANES), lambda b, h, t: (l, 0, wk0 + h)),
            pl.BlockSpec((None, CONV_W, hw), lambda b, h, t: (l, 0, wv0 + h)),
            pl.BlockSpec((1, tt, LANES), lambda b, h, t: (b, t, 0)),
            pl.BlockSpec((1, SUBLANES, tt), lambda b, h, t: (b, h, t)),
        ],
        out_specs=[o[0] for o in outs],
        out_shape=[o[1] for o in outs],
        scratch_shapes=[
            pltpu.VMEM((tt + SUBLANES, LANES), f32),
            pltpu.VMEM((tt + SUBLANES, LANES), f32),
            pltpu.VMEM((tt + SUBLANES, hw), f32),
        ],
        compiler_params=_cparams("parallel", "parallel", "parallel"),
        name="gdn_prep",
    )(P, P, P, P, P, P, cw, cw, cw, gcol, grow)


def _gdn_scan_kernel(u_ref, w_ref, qd_ref, kd_ref, aqk_ref, egl_ref, z_ref, nrm_ref, o_ref, s_ref, *, tt, nh):
    @pl.when(pl.program_id(2) == 0)
    def _():
        s_ref[...] = jnp.zeros_like(s_ref)

    nrm = nrm_ref[...]
    zeros = jnp.zeros((CHUNK, LANES), bf16)
    heads = range(nh)
    S = [s_ref[g] for g in heads]
    v_prev = [zeros for _ in heads]
    for c in range(tt // CHUNK):
        rows = slice(c * CHUNK, (c + 1) * CHUNK)
        ws_qs = [_dot(jnp.concatenate([w_ref[0, g, rows, :], qd_ref[0, g, rows, :]], axis=0),
                      S[g].astype(bf16)) for g in heads]
        v_new = [(u_ref[0, g, rows, :] - ws_qs[g][:CHUNK]).astype(bf16) for g in heads]
        v_pair = [jnp.concatenate([v_new[g], zeros] if c % 2 == 0 else [v_prev[g], v_new[g]], axis=0)
                  for g in heads]
        o = [ws_qs[g][CHUNK:] + _dot(aqk_ref[0, g, rows, :], v_pair[g]) for g in heads]
        S = [S[g] * egl_ref[0, g, c:c + 1, :] + _dot_tn(kd_ref[0, g, rows, :], v_new[g]) for g in heads]
        for g in heads:
            z = z_ref[0, rows, g * LANES:(g + 1) * LANES]
            o_ref[0, rows, g * LANES:(g + 1) * LANES] = (_rms(o[g]) * nrm * (z * _sigmoid(z))).astype(bf16)
        v_prev = v_new
    for g in heads:
        s_ref[g] = S[g]


def _gdn_scan(factors, P, nrm, l, off_z):
    u = factors[0]
    B, VH, T, _ = u.shape
    nh = _tile(VH, 8)
    tt = _tile(T, 512)
    cz = off_z // (nh * LANES)
    assert off_z % (nh * LANES) == 0 and nh % 2 == 0
    head_in = lambda: pl.BlockSpec((1, nh, tt, LANES), lambda b, h, t: (b, h, t, 0))
    return pl.pallas_call(
        functools.partial(_gdn_scan_kernel, tt=tt, nh=nh),
        grid=(B, VH // nh, T // tt),
        in_specs=[
            head_in(), head_in(), head_in(), head_in(), head_in(),
            pl.BlockSpec((1, nh, tt // CHUNK, LANES), lambda b, h, t: (b, h, t, 0)),
            pl.BlockSpec((1, tt, nh * LANES), lambda b, h, t: (b, t, cz + h)),
            pl.BlockSpec((None, 1, LANES), lambda b, h, t: (l, 0, 0)),
        ],
        out_specs=pl.BlockSpec((1, tt, nh * LANES), lambda b, h, t: (b, t, h)),
        out_shape=jax.ShapeDtypeStruct((B, T, VH * LANES), bf16),
        scratch_shapes=[pltpu.VMEM((nh, LANES, LANES), f32)],
        compiler_params=_cparams("parallel", "parallel", "arbitrary"),
        name="gdn_scan",
    )(*factors, P, nrm)


def _merge_kernel(x_ref, yl_ref, yd_ref, gl_ref, gd_ref, wl_ref, wd_ref, wo_ref, o_ref):
    @pl.when(pl.program_id(1) == 0)
    def _():
        o_ref[...] = x_ref[...]

    a = _dot(yl_ref[...], wl_ref[...])
    b = _dot(yd_ref[...], wd_ref[...])
    merged = (_sigmoid(gl_ref[...]) * a + _sigmoid(gd_ref[...]) * b).astype(bf16)
    o_ref[...] += _dot(merged, wo_ref[...])


def _merge(x, yl, yd, G, wl, wd, wo, l):
    M, D = x.shape
    R, V = yl.shape[1], yd.shape[1]
    tm, tn = _tile(M, 512), _tile(D, 512)
    nd = D // tn
    return pl.pallas_call(
        _merge_kernel,
        grid=(M // tm, nd),
        in_specs=[
            pl.BlockSpec((tm, D), lambda i, j: (i, 0)),
            pl.BlockSpec((tm, R), lambda i, j: (i, 0)),
            pl.BlockSpec((tm, V), lambda i, j: (i, 0)),
            pl.BlockSpec((tm, tn), lambda i, j: (i, j)),
            pl.BlockSpec((tm, tn), lambda i, j: (i, nd + j)),
            pl.BlockSpec((None, R, tn), lambda i, j: (l, 0, j)),
            pl.BlockSpec((None, V, tn), lambda i, j: (l, 0, j)),
            pl.BlockSpec((None, tn, D), lambda i, j: (l, j, 0)),
        ],
        out_specs=pl.BlockSpec((tm, D), lambda i, j: (i, 0)),
        out_shape=jax.ShapeDtypeStruct((M, D), f32),
        compiler_params=_cparams("parallel", "arbitrary"),
        name="merge",
    )(x, yl, yd, G, G, wl, wd, wo)


def _final_norm_kernel(x_ref, g_ref, o_ref):
    o_ref[...] = _rms(x_ref[...]) * g_ref[...]


def _final_norm(x, g):
    M, D = x.shape
    tm = _tile(M, 512)
    return pl.pallas_call(
        _final_norm_kernel,
        grid=(M // tm,),
        in_specs=[pl.BlockSpec((tm, D), lambda i: (i, 0)), pl.BlockSpec((1, D), lambda i: (0, 0))],
        out_specs=pl.BlockSpec((tm, D), lambda i: (i, 0)),
        out_shape=jax.ShapeDtypeStruct((M, D), f32),
        compiler_params=_cparams("parallel"),
        name="final_norm",
    )(x, g)


def kernel(x, ffn1_norm, ffn1_w_in, ffn1_w_out, mix_norm, w_in, lru_conv_w, lru_conv_b, lru_gate_a_w, lru_gate_a_b, lru_gate_x_w, lru_gate_x_b, lru_lambda, dn_conv_w, dn_A_log, dn_dt_bias, dn_norm, w_branch_lru, w_branch_dn, w_out, ffn2_norm, ffn2_w_in, ffn2_w_out, final_norm):
    B, T, D = x.shape
    depth = ffn1_norm.shape[0]
    R = lru_conv_w.shape[-1]
    VH = dn_A_log.shape[-1]
    V = VH * LANES
    Q = (dn_conv_w.shape[-1] - V) // 2
    assert Q // LANES == VH // 2 and 2 * VH <= LANES and T % PAIR == 0
    off_q = 2 * R
    off_k = off_q + Q
    off_v = off_k + Q
    off_z = off_v + V
    off_b = off_z + V
    off_gates = off_b + 2 * VH
    assert w_in.shape[-1] == off_gates + 2 * D

    row = lambda p: p[:, None, :]
    w_main = w_in.astype(bf16)
    w_ba = jnp.pad(w_in[:, :, off_b:off_gates], ((0, 0), (0, 0), (0, LANES - 2 * VH))).astype(bf16)
    w_gates = w_in[:, :, off_gates:].astype(bf16)
    ffn1_wi, ffn1_wo = ffn1_w_in.astype(bf16), ffn1_w_out.astype(bf16)
    ffn2_wi, ffn2_wo = ffn2_w_in.astype(bf16), ffn2_w_out.astype(bf16)
    wl, wd, wo = w_branch_lru.astype(bf16), w_branch_dn.astype(bf16), w_out.astype(bf16)
    wa, wx = lru_gate_a_w.astype(bf16), lru_gate_x_w.astype(bf16)
    pad_a = lambda p: jnp.pad(p, ((0, 0), (VH, LANES - 2 * VH)))[:, None, :]
    alog, dtb = pad_a(dn_A_log), pad_a(dn_dt_bias)

    xm = x.reshape(B * T, D)
    for l in range(depth):
        xm = _ffn(xm, row(ffn1_norm), ffn1_wi, ffn1_wo, l, 0.5)
        P = _norm_proj(xm, row(mix_norm), w_main, l, "in_proj", n_cols=off_b).reshape(B, T, -1)
        ba = _norm_proj(xm, row(mix_norm), w_ba, l, "ba_proj").reshape(B, T, LANES)
        gates = _norm_proj(xm, row(mix_norm), w_gates, l, "gate_proj")
        y_lru = _lru(P, lru_conv_w, row(lru_conv_b), wa, row(lru_gate_a_b), wx, row(lru_gate_x_b),
                     row(lru_lambda), l, R)
        gcol, grow = _dn_gates(ba, alog, dtb, l, VH)
        factors = _gdn_prep(P, dn_conv_w, gcol, grow, l, (off_q, off_k, off_v, off_z, Q, V))
        y_dn = _gdn_scan(factors, P, row(dn_norm), l, off_z)
        xm = _merge(xm, y_lru.reshape(B * T, R), y_dn.reshape(B * T, V), gates, wl, wd, wo, l)
        xm = _ffn(xm, row(ffn2_norm), ffn2_wi, ffn2_wo, l, 0.5)
    return _final_norm(xm, final_norm[None, :]).reshape(B, T, D)
```

```python
import functools
import math

import numpy as np
import jax
import jax.numpy as jnp
from jax import lax
from jax.experimental import pallas as pl
from jax.experimental.pallas import tpu as pltpu

NORM_EPS = 1e-6
CHUNK = 64
PAIR = 2 * CHUNK
GROUP = 16
CONV_W = 4
LRU_C = 8.0
LANES = 128
SUBLANES = 8
VMEM_LIMIT = 56 * 2**20

f32 = jnp.float32
bf16 = jnp.bfloat16


def _cparams(*sem):
    return pltpu.CompilerParams(dimension_semantics=sem, vmem_limit_bytes=VMEM_LIMIT)


def _tile(n, pref):
    t = min(n, pref)
    assert n % t == 0, (n, pref)
    return t


def _sigmoid(x):
    return 0.5 * jnp.tanh(0.5 * x) + 0.5


def _softplus(x):
    return jnp.maximum(x, 0.0) + jnp.log1p(jnp.exp(-jnp.abs(x)))


def _rms(x):
    return x * lax.rsqrt(jnp.mean(x * x, axis=-1, keepdims=True) + NORM_EPS)


def _dot(a, b):
    return jnp.dot(a, b, preferred_element_type=f32)


def _dot_nt(a, b):
    return lax.dot_general(a, b, (((1,), (1,)), ((), ())), preferred_element_type=f32)


def _dot_tn(a, b):
    return lax.dot_general(a, b, (((0,), (0,)), ((), ())), preferred_element_type=f32)


def _ffn_kernel(x_ref, g_ref, wg_ref, wu_ref, wo_ref, *rest, scale, final):
    fg_ref, o_ref, h_ref = rest if final else (None,) + rest

    @pl.when(pl.program_id(1) == 0)
    def _():
        x = x_ref[...]
        h_ref[...] = (_rms(x) * g_ref[...]).astype(bf16)
        o_ref[...] = x

    h = h_ref[...]
    gate = _dot(h, wg_ref[...])
    up = _dot(h, wu_ref[...])
    act = (gate * _sigmoid(gate) * up).astype(bf16)
    o_ref[...] += scale * _dot(act, wo_ref[...])

    if final:
        @pl.when(pl.program_id(1) == pl.num_programs(1) - 1)
        def _():
            o_ref[...] = _rms(o_ref[...]) * fg_ref[...]


def _ffn(x, g, w_in, w_out, l, scale, final_g=None):
    M, D = x.shape
    final = final_g is not None
    F = w_out.shape[1]
    tm, tf = _tile(M, 1024), _tile(F, 512)
    nf = F // tf
    in_specs = [
        pl.BlockSpec((tm, D), lambda i, j: (i, 0)),
        pl.BlockSpec((None, 1, D), lambda i, j: (l, 0, 0)),
        pl.BlockSpec((None, D, tf), lambda i, j: (l, 0, j)),
        pl.BlockSpec((None, D, tf), lambda i, j: (l, 0, j + nf)),
        pl.BlockSpec((None, tf, D), lambda i, j: (l, j, 0)),
    ]
    args = [x, g, w_in, w_in, w_out]
    if final:
        in_specs.append(pl.BlockSpec((1, D), lambda i, j: (0, 0)))
        args.append(final_g)
    return pl.pallas_call(
        functools.partial(_ffn_kernel, scale=scale, final=final),
        grid=(M // tm, nf),
        in_specs=in_specs,
        out_specs=pl.BlockSpec((tm, D), lambda i, j: (i, 0)),
        out_shape=jax.ShapeDtypeStruct((M, D), f32),
        scratch_shapes=[pltpu.VMEM((tm, D), bf16)],
        compiler_params=_cparams("parallel", "arbitrary"),
        name="ffn",
    )(*args)


def _norm_proj_kernel(x_ref, g_ref, w_ref, o_ref, h_ref):
    @pl.when(pl.program_id(1) == 0)
    def _():
        h_ref[...] = (_rms(x_ref[...]) * g_ref[...]).astype(bf16)

    o_ref[...] = _dot(h_ref[...], w_ref[...])


def _norm_proj(x, g, w, l, name, n_cols=None):
    M, D = x.shape
    N = w.shape[2] if n_cols is None else n_cols
    tm, tn = _tile(M, 1024), _tile(N, 1024)
    return pl.pallas_call(
        _norm_proj_kernel,
        grid=(M // tm, N // tn),
        in_specs=[
            pl.BlockSpec((tm, D), lambda i, j: (i, 0)),
            pl.BlockSpec((None, 1, D), lambda i, j: (l, 0, 0)),
            pl.BlockSpec((None, D, tn), lambda i, j: (l, 0, j)),
        ],
        out_specs=pl.BlockSpec((tm, tn), lambda i, j: (i, j)),
        out_shape=jax.ShapeDtypeStruct((M, N), f32),
        scratch_shapes=[pltpu.VMEM((tm, D), bf16)],
        compiler_params=_cparams("parallel", "arbitrary"),
        name=name,
    )(x, g, w)


def _lru_kernel(xr_ref, yr_ref, cw_ref, cb_ref, wa_ref, ba_ref, wx_ref, bx_ref, lam_ref,
                o_ref, xe_ref, hc_ref, a_ref, u_ref, *, tt, tc):
    t = pl.program_id(2)

    @pl.when(t == 0)
    def _():
        xe_ref[0:SUBLANES, :] = jnp.zeros((SUBLANES, tc), f32)
        hc_ref[...] = jnp.zeros_like(hc_ref)

    @pl.when(t > 0)
    def _():
        xe_ref[0:SUBLANES, :] = xe_ref[tt:tt + SUBLANES, :]

    x = xr_ref[0]
    xe_ref[SUBLANES:SUBLANES + tt, :] = x
    cw = cw_ref[...]
    xc = cw[3:4] * x + cb_ref[...]
    for s in range(1, CONV_W):
        xc = xc + cw[3 - s:4 - s] * xe_ref[SUBLANES - s:SUBLANES - s + tt, :]

    ra, ri = [], []
    for blk in range(tc // LANES):
        xb = xc[:, blk * LANES:(blk + 1) * LANES].astype(bf16)
        ra.append(_dot(xb, wa_ref[blk]))
        ri.append(_dot(xb, wx_ref[blk]))
    r = _sigmoid(jnp.concatenate(ra, axis=1) + ba_ref[...])
    i = _sigmoid(jnp.concatenate(ri, axis=1) + bx_ref[...])
    log_a = (-LRU_C) * r * _softplus(-lam_ref[...])
    a = jnp.exp(log_a)
    u = xc * i * jnp.sqrt(-jnp.tanh(log_a) * (a * a + 1.0))
    a_ref[...] = a
    u_ref[...] = u

    row = lax.broadcasted_iota(jnp.int32, (SUBLANES, tc), 0)

    def group(g, h):
        r0 = pl.multiple_of(g * SUBLANES, SUBLANES)
        A = a_ref[pl.ds(r0, SUBLANES), :]
        B = u_ref[pl.ds(r0, SUBLANES), :]
        for s in (1, 2, 4):
            keep = row >= s
            B = jnp.where(keep, A * pltpu.roll(B, s, 0) + B, B)
            A = jnp.where(keep, A * pltpu.roll(A, s, 0), A)
        H = A * h + B
        u_ref[pl.ds(r0, SUBLANES), :] = H
        return H[SUBLANES - 1:SUBLANES, :]

    hc_ref[0:1, :] = lax.fori_loop(0, tt // SUBLANES, group, hc_ref[0:1, :], unroll=4)

    y = yr_ref[0]
    gelu = 0.5 * y * (1.0 + jnp.tanh(math.sqrt(2.0 / math.pi) * (y + 0.044715 * (y * y * y))))
    o_ref[0] = (u_ref[...] * gelu).astype(bf16)


def _lru(P, cw, cb, wa, ba, wx, bx, lam, l, R):
    B, T, _ = P.shape
    tt, tc = _tile(T, 512), _tile(R, 512)
    nc = R // tc
    nb = tc // LANES
    vec = lambda: pl.BlockSpec((None, 1, tc), lambda b, c, t: (l, 0, c))
    gw = lambda: pl.BlockSpec((None, nb, LANES, LANES), lambda b, c, t: (l, c, 0, 0))
    return pl.pallas_call(
        functools.partial(_lru_kernel, tt=tt, tc=tc),
        grid=(B, nc, T // tt),
        in_specs=[
            pl.BlockSpec((1, tt, tc), lambda b, c, t: (b, t, c)),
            pl.BlockSpec((1, tt, tc), lambda b, c, t: (b, t, c + nc)),
            pl.BlockSpec((None, CONV_W, tc), lambda b, c, t: (l, 0, c)),
            vec(), gw(), vec(), gw(), vec(), vec(),
        ],
        out_specs=pl.BlockSpec((1, tt, tc), lambda b, c, t: (b, t, c)),
        out_shape=jax.ShapeDtypeStruct((B, T, R), bf16),
        scratch_shapes=[
            pltpu.VMEM((tt + SUBLANES, tc), f32),
            pltpu.VMEM((SUBLANES, tc), f32),
            pltpu.VMEM((tt, tc), f32),
            pltpu.VMEM((tt, tc), f32),
        ],
        compiler_params=_cparams("parallel", "parallel", "arbitrary"),
        name="lru",
    )(P, P, cw, cb, wa, ba, wx, bx, lam)


def _dn_gate_kernel(ba_ref, alog_ref, dtb_ref, tril_ref, prow_ref, gcol_ref, grow_ref, *, vh):
    ba = ba_ref[0]
    lane = lax.broadcasted_iota(jnp.int32, ba.shape, 1)
    beta = _sigmoid(ba)
    g = -jnp.exp(alog_ref[...]) * _softplus(ba + dtb_ref[...])
    gcum = jnp.dot(tril_ref[...], g, preferred_element_type=f32, precision=lax.Precision.HIGHEST)
    vals = jnp.where(lane < vh, beta, gcum)
    gcol_ref[0] = vals
    grow_ref[0] = lax.dot_general(prow_ref[...], vals, (((1,), (1,)), ((), ())),
                                  preferred_element_type=f32, precision=lax.Precision.HIGHEST)


def _dn_gate_consts(vh, tt):
    kh = vh // 2
    prow = np.zeros((kh * SUBLANES, LANES), np.float32)
    for p in range(kh):
        for m, src in enumerate((2 * p, 2 * p + 1, vh + 2 * p, vh + 2 * p + 1)):
            prow[p * SUBLANES + m, src] = 1.0
    idx = np.arange(tt)
    tril = ((idx[:, None] // CHUNK == idx[None, :] // CHUNK) & (idx[None, :] <= idx[:, None])).astype(np.float32)
    return jnp.asarray(tril), jnp.asarray(prow)


def _dn_gates(ba, alog, dtb, l, vh):
    B, T, _ = ba.shape
    kh = vh // 2
    tt = _tile(T, 256)
    tril, prow = _dn_gate_consts(vh, tt)
    full = lambda a: pl.BlockSpec(a.shape, lambda b, t: (0,) * a.ndim)
    return pl.pallas_call(
        functools.partial(_dn_gate_kernel, vh=vh),
        grid=(B, T // tt),
        in_specs=[
            pl.BlockSpec((1, tt, LANES), lambda b, t: (b, t, 0)),
            pl.BlockSpec((None, 1, LANES), lambda b, t: (l, 0, 0)),
            pl.BlockSpec((None, 1, LANES), lambda b, t: (l, 0, 0)),
            full(tril), full(prow),
        ],
        out_specs=[
            pl.BlockSpec((1, tt, LANES), lambda b, t: (b, t, 0)),
            pl.BlockSpec((1, kh * SUBLANES, tt), lambda b, t: (b, 0, t)),
        ],
        out_shape=[
            jax.ShapeDtypeStruct((B, T, LANES), f32),
            jax.ShapeDtypeStruct((B, kh * SUBLANES, T), f32),
        ],
        compiler_params=_cparams("parallel", "parallel"),
        name="dn_gates",
    )(ba, alog, dtb, tril, prow)


def _conv_silu(x_ref, halo_ref, w_ref, xe_ref, t, tt):
    xe_ref[0:SUBLANES, :] = jnp.where(t > 0, halo_ref[0], 0.0)
    x = x_ref[0]
    xe_ref[SUBLANES:SUBLANES + tt, :] = x
    w = w_ref[...]
    y = w[3:4] * x
    for s in range(1, CONV_W):
        y = y + w[3 - s:4 - s] * xe_ref[SUBLANES - s:SUBLANES - s + tt, :]
    return y * _sigmoid(y)


def _gdn_prep_kernel(q_ref, k_ref, v_ref, qh_ref, kh_ref, vh_ref, wq_ref, wk_ref, wv_ref, gcol_ref, grow_ref,
                     u_ref, w_ref, qd_ref, kd_ref, aqk_ref, egl_ref, qe_ref, ke_ref, ve_ref, *, tt, vh):
    t = pl.program_id(2)
    khead = pl.program_id(1)
    q = _conv_silu(q_ref, qh_ref, wq_ref, qe_ref, t, tt)
    k = _conv_silu(k_ref, kh_ref, wk_ref, ke_ref, t, tt)
    v = _conv_silu(v_ref, vh_ref, wv_ref, ve_ref, t, tt)
    q = q * lax.rsqrt(jnp.sum(q * q, axis=-1, keepdims=True) + NORM_EPS) * (LANES ** -0.5)
    k = k * lax.rsqrt(jnp.sum(k * k, axis=-1, keepdims=True) + NORM_EPS)

    ri = lax.broadcasted_iota(jnp.int32, (PAIR, PAIR), 0)
    ci = lax.broadcasted_iota(jnp.int32, (PAIR, PAIR), 1)
    same_chunk = (ri >= CHUNK) == (ci >= CHUNK)
    causal = same_chunk & (ci <= ri)
    strict = same_chunk & (ci < ri)
    eye = (ri == ci).astype(f32)
    first = lax.broadcasted_iota(jnp.int32, (PAIR, 1), 0) < CHUNK
    levels = int(math.log2(CHUNK)) - 2

    pairs = range(tt // PAIR)
    rows = [slice(p * PAIR, (p + 1) * PAIR) for p in pairs]
    kb = [k[r].astype(bf16) for r in rows]
    kk = [_dot_nt(kb[p], kb[p]) for p in pairs]
    qk = [_dot_nt(q[rows[p]].astype(bf16), kb[p]) for p in pairs]

    chains = [(p, j) for p in pairs for j in range(2)]
    lane = lax.broadcasted_iota(jnp.int32, (1, LANES), 1)

    def column(r, idx):
        return jnp.sum(jnp.where(lane == idx, gcol_ref[0, r, :], 0.0), axis=-1, keepdims=True)

    beta = [column(rows[p], 2 * khead + j) for p, j in chains]
    gc = [column(rows[p], vh + 2 * khead + j) for p, j in chains]
    decay = [jnp.where(causal, jnp.exp(gc[i] - grow_ref[0, 2 + j:3 + j, rows[p]]), 0.0)
             for i, (p, j) in enumerate(chains)]
    X = []
    for g0 in range(0, len(chains), GROUP):
        grp = range(g0, min(g0 + GROUP, len(chains)))
        N = [jnp.where(strict, -(beta[i] * kk[chains[i][0]] * decay[i]), 0.0) for i in grp]
        Xg = [eye + n for n in N]
        Nb = [n.astype(bf16) for n in N]
        P = [_dot(nb, nb) for nb in Nb]
        for _ in range(levels):
            Pb = [m.astype(bf16) for m in P]
            PX = [_dot(Pb[i], jnp.concatenate([Pb[i], Xg[i].astype(bf16)], axis=1)) for i in range(len(grp))]
            P = [m[:, :PAIR] for m in PX]
            Xg = [Xg[i] + PX[i][:, PAIR:] for i in range(len(grp))]
        X += [Xg[i] + _dot(P[i].astype(bf16), Xg[i].astype(bf16)) for i in range(len(grp))]

    for i, (p, j) in enumerate(chains):
        qp, kp = q[rows[p]], k[rows[p]]
        eg = jnp.exp(gc[i])
        vp = v[rows[p], j * LANES:(j + 1) * LANES]
        rhs = jnp.concatenate([vp * beta[i], kp * (beta[i] * eg)], axis=1).astype(bf16)
        sol = _dot(X[i].astype(bf16), rhs)
        g_last = jnp.where(first, gc[i][CHUNK - 1:CHUNK, :], gc[i][PAIR - 1:PAIR, :])
        u_ref[0, j, rows[p], :] = sol[:, :LANES]
        w_ref[0, j, rows[p], :] = sol[:, LANES:].astype(bf16)
        qd_ref[0, j, rows[p], :] = (qp * eg).astype(bf16)
        kd_ref[0, j, rows[p], :] = (kp * jnp.exp(g_last - gc[i])).astype(bf16)
        aqk_ref[0, j, rows[p], :] = (qk[p] * decay[i]).astype(bf16)
        for c, r in ((2 * p, CHUNK - 1), (2 * p + 1, PAIR - 1)):
            egl_ref[0, j, c:c + 1, :] = jnp.broadcast_to(jnp.exp(gc[i][r:r + 1, :]), (1, LANES))


def _gdn_prep(P, cw, gcol, grow, l, offs):
    B, T, _ = P.shape
    off_q, off_k, off_v, off_z, Q, V = offs
    KH = Q // LANES
    VH = 2 * KH
    tt = _tile(T, 1024)
    hw = 2 * LANES
    cq, ck, cv = off_q // LANES, off_k // LANES, off_v // hw
    wk0, wv0 = Q // LANES, (2 * Q) // hw
    hb = tt // SUBLANES
    prev = lambda t: jnp.maximum(t * hb - 1, 0)
    head_out = lambda dt: (pl.BlockSpec((1, 2, tt, LANES), lambda b, h, t: (b, h, t, 0)),
                           jax.ShapeDtypeStruct((B, VH, T, LANES), dt))
    outs = [head_out(f32)] + [head_out(bf16) for _ in range(4)]
    outs.append((pl.BlockSpec((1, 2, tt // CHUNK, LANES), lambda b, h, t: (b, h, t, 0)),
                 jax.ShapeDtypeStruct((B, VH, T // CHUNK, LANES), f32)))
    return pl.pallas_call(
        functools.partial(_gdn_prep_kernel, tt=tt, vh=VH),
        grid=(B, KH, T // tt),
        in_specs=[
            pl.BlockSpec((1, tt, LANES), lambda b, h, t: (b, t, cq + h)),
            pl.BlockSpec((1, tt, LANES), lambda b, h, t: (b, t, ck + h)),
            pl.BlockSpec((1, tt, hw), lambda b, h, t: (b, t, cv + h)),
            pl.BlockSpec((1, SUBLANES, LANES), lambda b, h, t: (b, prev(t), cq + h)),
            pl.BlockSpec((1, SUBLANES, LANES), lambda b, h, t: (b, prev(t), ck + h)),
            pl.BlockSpec((1, SUBLANES, hw), lambda b, h, t: (b, prev(t), cv + h)),
            pl.BlockSpec((None, CONV_W, LANES), lambda b, h, t: (l, 0, h)),
            pl.BlockSpec((None, CONV_W, LANES), lambda b, h, t: (l, 0, wk0 + h)),
            pl.BlockSpec((None, CONV_W, hw), lambda b, h, t: (l, 0, wv0 + h)),
            pl.BlockSpec((1, tt, LANES), lambda b, h, t: (b, t, 0)),
            pl.BlockSpec((1, SUBLANES, tt), lambda b, h, t: (b, h, t)),
        ],
        out_specs=[o[0] for o in outs],
        out_shape=[o[1] for o in outs],
        scratch_shapes=[
            pltpu.VMEM((tt + SUBLANES, LANES), f32),
            pltpu.VMEM((tt + SUBLANES, LANES), f32),
            pltpu.VMEM((tt + SUBLANES, hw), f32),
        ],
        compiler_params=_cparams("parallel", "parallel", "parallel"),
        name="gdn_prep",
    )(P, P, P, P, P, P, cw, cw, cw, gcol, grow)


def _gdn_scan_kernel(u_ref, w_ref, qd_ref, kd_ref, aqk_ref, egl_ref, z_ref, nrm_ref, o_ref, s_ref, *, tt, nh):
    @pl.when(pl.program_id(2) == 0)
    def _():
        s_ref[...] = jnp.zeros_like(s_ref)

    nrm = nrm_ref[...]
    zeros = jnp.zeros((CHUNK, LANES), bf16)
    heads = range(nh)
    S = [s_ref[g] for g in heads]
    v_prev = [zeros for _ in heads]
    for c in range(tt // CHUNK):
        rows = slice(c * CHUNK, (c + 1) * CHUNK)
        ws_qs = [_dot(jnp.concatenate([w_ref[0, g, rows, :], qd_ref[0, g, rows, :]], axis=0),
                      S[g].astype(bf16)) for g in heads]
        v_new = [(u_ref[0, g, rows, :] - ws_qs[g][:CHUNK]).astype(bf16) for g in heads]
        v_pair = [jnp.concatenate([v_new[g], zeros] if c % 2 == 0 else [v_prev[g], v_new[g]], axis=0)
                  for g in heads]
        o = [ws_qs[g][CHUNK:] + _dot(aqk_ref[0, g, rows, :], v_pair[g]) for g in heads]
        S = [S[g] * egl_ref[0, g, c:c + 1, :] + _dot_tn(kd_ref[0, g, rows, :], v_new[g]) for g in heads]
        for g in heads:
            z = z_ref[0, rows, g * LANES:(g + 1) * LANES]
            o_ref[0, rows, g * LANES:(g + 1) * LANES] = (_rms(o[g]) * nrm * (z * _sigmoid(z))).astype(bf16)
        v_prev = v_new
    for g in heads:
        s_ref[g] = S[g]


def _gdn_scan(factors, P, nrm, l, off_z):
    u = factors[0]
    B, VH, T, _ = u.shape
    nh = _tile(VH, 8)
    tt = _tile(T, 512)
    cz = off_z // (nh * LANES)
    assert off_z % (nh * LANES) == 0 and nh % 2 == 0
    head_in = lambda: pl.BlockSpec((1, nh, tt, LANES), lambda b, h, t: (b, h, t, 0))
    return pl.pallas_call(
        functools.partial(_gdn_scan_kernel, tt=tt, nh=nh),
        grid=(B, VH // nh, T // tt),
        in_specs=[
            head_in(), head_in(), head_in(), head_in(), head_in(),
            pl.BlockSpec((1, nh, tt // CHUNK, LANES), lambda b, h, t: (b, h, t, 0)),
            pl.BlockSpec((1, tt, nh * LANES), lambda b, h, t: (b, t, cz + h)),
            pl.BlockSpec((None, 1, LANES), lambda b, h, t: (l, 0, 0)),
        ],
        out_specs=pl.BlockSpec((1, tt, nh * LANES), lambda b, h, t: (b, t, h)),
        out_shape=jax.ShapeDtypeStruct((B, T, VH * LANES), bf16),
        scratch_shapes=[pltpu.VMEM((nh, LANES, LANES), f32)],
        compiler_params=_cparams("parallel", "parallel", "arbitrary"),
        name="gdn_scan",
    )(*factors, P, nrm)


def _merge_kernel(yl_ref, yd_ref, gl_ref, gd_ref, wl_ref, wd_ref, o_ref):
    a = _dot(yl_ref[...], wl_ref[...])
    b = _dot(yd_ref[...], wd_ref[...])
    o_ref[...] = (_sigmoid(gl_ref[...]) * a + _sigmoid(gd_ref[...]) * b).astype(bf16)


def _out_proj_kernel(x_ref, m_ref, wo_ref, o_ref):
    o_ref[...] = x_ref[...] + _dot(m_ref[...], wo_ref[...])


def _merge(x, yl, yd, G, wl, wd, wo, l):
    M, D = x.shape
    R, V = yl.shape[1], yd.shape[1]
    tm, tn = _tile(M, 1024), _tile(D, 512)
    nd = D // tn
    merged = pl.pallas_call(
        _merge_kernel,
        grid=(M // tm, nd),
        in_specs=[
            pl.BlockSpec((tm, R), lambda i, j: (i, 0)),
            pl.BlockSpec((tm, V), lambda i, j: (i, 0)),
            pl.BlockSpec((tm, tn), lambda i, j: (i, j)),
            pl.BlockSpec((tm, tn), lambda i, j: (i, nd + j)),
            pl.BlockSpec((None, R, tn), lambda i, j: (l, 0, j)),
            pl.BlockSpec((None, V, tn), lambda i, j: (l, 0, j)),
        ],
        out_specs=pl.BlockSpec((tm, tn), lambda i, j: (i, j)),
        out_shape=jax.ShapeDtypeStruct((M, D), bf16),
        compiler_params=_cparams("parallel", "arbitrary"),
        name="merge",
    )(yl, yd, G, G, wl, wd)
    return pl.pallas_call(
        _out_proj_kernel,
        grid=(M // tm, nd),
        in_specs=[
            pl.BlockSpec((tm, tn), lambda i, j: (i, j)),
            pl.BlockSpec((tm, D), lambda i, j: (i, 0)),
            pl.BlockSpec((None, D, tn), lambda i, j: (l, 0, j)),
        ],
        out_specs=pl.BlockSpec((tm, tn), lambda i, j: (i, j)),
        out_shape=jax.ShapeDtypeStruct((M, D), f32),
        compiler_params=_cparams("parallel", "arbitrary"),
        name="out_proj",
    )(x, merged, wo)


def kernel(x, ffn1_norm, ffn1_w_in, ffn1_w_out, mix_norm, w_in, lru_conv_w, lru_conv_b, lru_gate_a_w, lru_gate_a_b, lru_gate_x_w, lru_gate_x_b, lru_lambda, dn_conv_w, dn_A_log, dn_dt_bias, dn_norm, w_branch_lru, w_branch_dn, w_out, ffn2_norm, ffn2_w_in, ffn2_w_out, final_norm):
    B, T, D = x.shape
    depth = ffn1_norm.shape[0]
    R = lru_conv_w.shape[-1]
    VH = dn_A_log.shape[-1]
    V = VH * LANES
    Q = (dn_conv_w.shape[-1] - V) // 2
    assert Q // LANES == VH // 2 and 2 * VH <= LANES and T % PAIR == 0
    off_q = 2 * R
    off_k = off_q + Q
    off_v = off_k + Q
    off_z = off_v + V
    off_b = off_z + V
    off_gates = off_b + 2 * VH
    assert w_in.shape[-1] == off_gates + 2 * D

    row = lambda p: p[:, None, :]
    w_main = w_in.astype(bf16)
    w_ba = jnp.pad(w_in[:, :, off_b:off_gates], ((0, 0), (0, 0), (0, LANES - 2 * VH))).astype(bf16)
    w_gates = w_in[:, :, off_gates:].astype(bf16)
    ffn1_wi, ffn1_wo = ffn1_w_in.astype(bf16), ffn1_w_out.astype(bf16)
    ffn2_wi, ffn2_wo = ffn2_w_in.astype(bf16), ffn2_w_out.astype(bf16)
    wl, wd, wo = w_branch_lru.astype(bf16), w_branch_dn.astype(bf16), w_out.astype(bf16)
    wa, wx = lru_gate_a_w.astype(bf16), lru_gate_x_w.astype(bf16)
    pad_a = lambda p: jnp.pad(p, ((0, 0), (VH, LANES - 2 * VH)))[:, None, :]
    alog, dtb = pad_a(dn_A_log), pad_a(dn_dt_bias)

    xm = x.reshape(B * T, D)
    for l in range(depth):
        xm = _ffn(xm, row(ffn1_norm), ffn1_wi, ffn1_wo, l, 0.5)
        P = _norm_proj(xm, row(mix_norm), w_main, l, "in_proj", n_cols=off_b).reshape(B, T, -1)
        ba = _norm_proj(xm, row(mix_norm), w_ba, l, "ba_proj").reshape(B, T, LANES)
        gates = _norm_proj(xm, row(mix_norm), w_gates, l, "gate_proj")
        y_lru = _lru(P, lru_conv_w, row(lru_conv_b), wa, row(lru_gate_a_b), wx, row(lru_gate_x_b),
                     row(lru_lambda), l, R)
        gcol, grow = _dn_gates(ba, alog, dtb, l, VH)
        factors = _gdn_prep(P, dn_conv_w, gcol, grow, l, (off_q, off_k, off_v, off_z, Q, V))
        y_dn = _gdn_scan(factors, P, row(dn_norm), l, off_z)
        xm = _merge(xm, y_lru.reshape(B * T, R), y_dn.reshape(B * T, V), gates, wl, wd, wo, l)
        xm = _ffn(xm, row(ffn2_norm), ffn2_wi, ffn2_wo, l, 0.5,
                  final_g=final_norm[None, :] if l == depth - 1 else None)
    return xm.reshape(B, T, D)
```

```python
import functools
import math

import numpy as np
import jax
import jax.numpy as jnp
from jax import lax
from jax.experimental import pallas as pl
from jax.experimental.pallas import tpu as pltpu

NORM_EPS = 1e-6
CHUNK = 64
PAIR = 2 * CHUNK
GROUP = 16
CONV_W = 4
LRU_C = 8.0
LANES = 128
SUBLANES = 8
VMEM_LIMIT = 56 * 2**20

f32 = jnp.float32
bf16 = jnp.bfloat16


def _cparams(*sem):
    return pltpu.CompilerParams(dimension_semantics=sem, vmem_limit_bytes=VMEM_LIMIT)


def _tile(n, pref):
    t = min(n, pref)
    assert n % t == 0, (n, pref)
    return t


def _sigmoid(x):
    return 0.5 * jnp.tanh(0.5 * x) + 0.5


def _softplus(x):
    return jnp.maximum(x, 0.0) + jnp.log1p(jnp.exp(-jnp.abs(x)))


def _rms(x):
    return x * lax.rsqrt(jnp.mean(x * x, axis=-1, keepdims=True) + NORM_EPS)


def _dot(a, b):
    return jnp.dot(a, b, preferred_element_type=f32)


def _dot_nt(a, b):
    return lax.dot_general(a, b, (((1,), (1,)), ((), ())), preferred_element_type=f32)


def _dot_tn(a, b):
    return lax.dot_general(a, b, (((0,), (0,)), ((), ())), preferred_element_type=f32)


def _ffn_kernel(x_ref, g_ref, wg_ref, wu_ref, wo_ref, *rest, scale, final):
    fg_ref, o_ref, h_ref = rest if final else (None,) + rest

    @pl.when(pl.program_id(1) == 0)
    def _():
        x = x_ref[...]
        h_ref[...] = (_rms(x) * g_ref[...]).astype(bf16)
        o_ref[...] = x

    h = h_ref[...]
    gate = _dot(h, wg_ref[...])
    up = _dot(h, wu_ref[...])
    act = (gate * _sigmoid(gate) * up).astype(bf16)
    o_ref[...] += scale * _dot(act, wo_ref[...])

    if final:
        @pl.when(pl.program_id(1) == pl.num_programs(1) - 1)
        def _():
            o_ref[...] = _rms(o_ref[...]) * fg_ref[...]


def _ffn(x, g, w_in, w_out, l, scale, final_g=None):
    M, D = x.shape
    final = final_g is not None
    F = w_out.shape[1]
    tm, tf = _tile(M, 1024), _tile(F, 512)
    nf = F // tf
    in_specs = [
        pl.BlockSpec((tm, D), lambda i, j: (i, 0)),
        pl.BlockSpec((None, 1, D), lambda i, j: (l, 0, 0)),
        pl.BlockSpec((None, D, tf), lambda i, j: (l, 0, j)),
        pl.BlockSpec((None, D, tf), lambda i, j: (l, 0, j + nf)),
        pl.BlockSpec((None, tf, D), lambda i, j: (l, j, 0)),
    ]
    args = [x, g, w_in, w_in, w_out]
    if final:
        in_specs.append(pl.BlockSpec((1, D), lambda i, j: (0, 0)))
        args.append(final_g)
    return pl.pallas_call(
        functools.partial(_ffn_kernel, scale=scale, final=final),
        grid=(M // tm, nf),
        in_specs=in_specs,
        out_specs=pl.BlockSpec((tm, D), lambda i, j: (i, 0)),
        out_shape=jax.ShapeDtypeStruct((M, D), f32),
        scratch_shapes=[pltpu.VMEM((tm, D), bf16)],
        compiler_params=_cparams("parallel", "arbitrary"),
        name="ffn",
    )(*args)


def _norm_proj_kernel(x_ref, g_ref, w_ref, o_ref, h_ref):
    @pl.when(pl.program_id(1) == 0)
    def _():
        h_ref[...] = (_rms(x_ref[...]) * g_ref[...]).astype(bf16)

    o_ref[...] = _dot(h_ref[...], w_ref[...])


def _norm_proj(x, g, w, l, name, n_cols=None):
    M, D = x.shape
    N = w.shape[2] if n_cols is None else n_cols
    tm, tn = _tile(M, 1024), _tile(N, 1024)
    return pl.pallas_call(
        _norm_proj_kernel,
        grid=(M // tm, N // tn),
        in_specs=[
            pl.BlockSpec((tm, D), lambda i, j: (i, 0)),
            pl.BlockSpec((None, 1, D), lambda i, j: (l, 0, 0)),
            pl.BlockSpec((None, D, tn), lambda i, j: (l, 0, j)),
        ],
        out_specs=pl.BlockSpec((tm, tn), lambda i, j: (i, j)),
        out_shape=jax.ShapeDtypeStruct((M, N), f32),
        scratch_shapes=[pltpu.VMEM((tm, D), bf16)],
        compiler_params=_cparams("parallel", "arbitrary"),
        name=name,
    )(x, g, w)


def _lru_kernel(xr_ref, yr_ref, cw_ref, cb_ref, wa_ref, ba_ref, wx_ref, bx_ref, lam_ref,
                o_ref, xe_ref, hc_ref, a_ref, u_ref, *, tt, tc):
    t = pl.program_id(2)

    @pl.when(t == 0)
    def _():
        xe_ref[0:SUBLANES, :] = jnp.zeros((SUBLANES, tc), f32)
        hc_ref[...] = jnp.zeros_like(hc_ref)

    @pl.when(t > 0)
    def _():
        xe_ref[0:SUBLANES, :] = xe_ref[tt:tt + SUBLANES, :]

    x = xr_ref[0]
    xe_ref[SUBLANES:SUBLANES + tt, :] = x
    cw = cw_ref[...]
    xc = cw[3:4] * x + cb_ref[...]
    for s in range(1, CONV_W):
        xc = xc + cw[3 - s:4 - s] * xe_ref[SUBLANES - s:SUBLANES - s + tt, :]

    ra, ri = [], []
    for blk in range(tc // LANES):
        xb = xc[:, blk * LANES:(blk + 1) * LANES].astype(bf16)
        ra.append(_dot(xb, wa_ref[blk]))
        ri.append(_dot(xb, wx_ref[blk]))
    r = _sigmoid(jnp.concatenate(ra, axis=1) + ba_ref[...])
    i = _sigmoid(jnp.concatenate(ri, axis=1) + bx_ref[...])
    log_a = (-LRU_C) * r * _softplus(-lam_ref[...])
    a = jnp.exp(log_a)
    u = xc * i * jnp.exp(0.5 * jnp.log(-jnp.tanh(log_a) * (a * a + 1.0)))
    a_ref[...] = a
    u_ref[...] = u

    row = lax.broadcasted_iota(jnp.int32, (SUBLANES, tc), 0)

    def group(g, h):
        r0 = pl.multiple_of(g * SUBLANES, SUBLANES)
        A = a_ref[pl.ds(r0, SUBLANES), :]
        B = u_ref[pl.ds(r0, SUBLANES), :]
        for s in (1, 2, 4):
            keep = row >= s
            B = jnp.where(keep, A * pltpu.roll(B, s, 0) + B, B)
            A = jnp.where(keep, A * pltpu.roll(A, s, 0), A)
        H = A * h + B
        u_ref[pl.ds(r0, SUBLANES), :] = H
        return H[SUBLANES - 1:SUBLANES, :]

    hc_ref[0:1, :] = lax.fori_loop(0, tt // SUBLANES, group, hc_ref[0:1, :], unroll=4)

    y = yr_ref[0]
    gelu = 0.5 * y * (1.0 + jnp.tanh(math.sqrt(2.0 / math.pi) * (y + 0.044715 * (y * y * y))))
    o_ref[0] = (u_ref[...] * gelu).astype(bf16)


def _lru(P, cw, cb, wa, ba, wx, bx, lam, l, R):
    B, T, _ = P.shape
    tt, tc = _tile(T, 512), _tile(R, 512)
    nc = R // tc
    nb = tc // LANES
    vec = lambda: pl.BlockSpec((None, 1, tc), lambda b, c, t: (l, 0, c))
    gw = lambda: pl.BlockSpec((None, nb, LANES, LANES), lambda b, c, t: (l, c, 0, 0))
    return pl.pallas_call(
        functools.partial(_lru_kernel, tt=tt, tc=tc),
        grid=(B, nc, T // tt),
        in_specs=[
            pl.BlockSpec((1, tt, tc), lambda b, c, t: (b, t, c)),
            pl.BlockSpec((1, tt, tc), lambda b, c, t: (b, t, c + nc)),
            pl.BlockSpec((None, CONV_W, tc), lambda b, c, t: (l, 0, c)),
            vec(), gw(), vec(), gw(), vec(), vec(),
        ],
        out_specs=pl.BlockSpec((1, tt, tc), lambda b, c, t: (b, t, c)),
        out_shape=jax.ShapeDtypeStruct((B, T, R), bf16),
        scratch_shapes=[
            pltpu.VMEM((tt + SUBLANES, tc), f32),
            pltpu.VMEM((SUBLANES, tc), f32),
            pltpu.VMEM((tt, tc), f32),
            pltpu.VMEM((tt, tc), f32),
        ],
        compiler_params=_cparams("parallel", "parallel", "arbitrary"),
        name="lru",
    )(P, P, cw, cb, wa, ba, wx, bx, lam)


def _dn_gate_kernel(ba_ref, alog_ref, dtb_ref, tril_ref, prow_ref, gcol_ref, grow_ref, *, vh):
    ba = ba_ref[0]
    lane = lax.broadcasted_iota(jnp.int32, ba.shape, 1)
    beta = _sigmoid(ba)
    g = -jnp.exp(alog_ref[...]) * _softplus(ba + dtb_ref[...])
    gcum = jnp.dot(tril_ref[...], g, preferred_element_type=f32, precision=lax.Precision.HIGHEST)
    vals = jnp.where(lane < vh, beta, gcum)
    gcol_ref[0] = vals
    grow_ref[0] = lax.dot_general(prow_ref[...], vals, (((1,), (1,)), ((), ())),
                                  preferred_element_type=f32, precision=lax.Precision.HIGHEST)


def _dn_gate_consts(vh, tt):
    kh = vh // 2
    prow = np.zeros((kh * SUBLANES, LANES), np.float32)
    for p in range(kh):
        for m, src in enumerate((2 * p, 2 * p + 1, vh + 2 * p, vh + 2 * p + 1)):
            prow[p * SUBLANES + m, src] = 1.0
    idx = np.arange(tt)
    tril = ((idx[:, None] // CHUNK == idx[None, :] // CHUNK) & (idx[None, :] <= idx[:, None])).astype(np.float32)
    return jnp.asarray(tril), jnp.asarray(prow)


def _dn_gates(ba, alog, dtb, l, vh):
    B, T, _ = ba.shape
    kh = vh // 2
    tt = _tile(T, 256)
    tril, prow = _dn_gate_consts(vh, tt)
    full = lambda a: pl.BlockSpec(a.shape, lambda b, t: (0,) * a.ndim)
    return pl.pallas_call(
        functools.partial(_dn_gate_kernel, vh=vh),
        grid=(B, T // tt),
        in_specs=[
            pl.BlockSpec((1, tt, LANES), lambda b, t: (b, t, 0)),
            pl.BlockSpec((None, 1, LANES), lambda b, t: (l, 0, 0)),
            pl.BlockSpec((None, 1, LANES), lambda b, t: (l, 0, 0)),
            full(tril), full(prow),
        ],
        out_specs=[
            pl.BlockSpec((1, tt, LANES), lambda b, t: (b, t, 0)),
            pl.BlockSpec((1, kh * SUBLANES, tt), lambda b, t: (b, 0, t)),
        ],
        out_shape=[
            jax.ShapeDtypeStruct((B, T, LANES), f32),
            jax.ShapeDtypeStruct((B, kh * SUBLANES, T), f32),
        ],
        compiler_params=_cparams("parallel", "parallel"),
        name="dn_gates",
    )(ba, alog, dtb, tril, prow)


def _conv_silu(x_ref, halo_ref, w_ref, xe_ref, t, tt):
    xe_ref[0:SUBLANES, :] = jnp.where(t > 0, halo_ref[0], 0.0)
    x = x_ref[0]
    xe_ref[SUBLANES:SUBLANES + tt, :] = x
    w = w_ref[...]
    y = w[3:4] * x
    for s in range(1, CONV_W):
        y = y + w[3 - s:4 - s] * xe_ref[SUBLANES - s:SUBLANES - s + tt, :]
    return y * _sigmoid(y)


def _gdn_prep_kernel(q_ref, k_ref, v_ref, qh_ref, kh_ref, vh_ref, wq_ref, wk_ref, wv_ref, gcol_ref, grow_ref,
                     u_ref, w_ref, qd_ref, kd_ref, aqk_ref, egl_ref, qe_ref, ke_ref, ve_ref, *, tt, vh):
    t = pl.program_id(2)
    khead = pl.program_id(1)
    q = _conv_silu(q_ref, qh_ref, wq_ref, qe_ref, t, tt)
    k = _conv_silu(k_ref, kh_ref, wk_ref, ke_ref, t, tt)
    v = _conv_silu(v_ref, vh_ref, wv_ref, ve_ref, t, tt)
    q = q * lax.rsqrt(jnp.sum(q * q, axis=-1, keepdims=True) + NORM_EPS) * (LANES ** -0.5)
    k = k * lax.rsqrt(jnp.sum(k * k, axis=-1, keepdims=True) + NORM_EPS)

    ri = lax.broadcasted_iota(jnp.int32, (PAIR, PAIR), 0)
    ci = lax.broadcasted_iota(jnp.int32, (PAIR, PAIR), 1)
    same_chunk = (ri >= CHUNK) == (ci >= CHUNK)
    causal = same_chunk & (ci <= ri)
    strict = same_chunk & (ci < ri)
    eye = (ri == ci).astype(f32)
    first = lax.broadcasted_iota(jnp.int32, (PAIR, 1), 0) < CHUNK
    levels = int(math.log2(CHUNK)) - 2

    pairs = range(tt // PAIR)
    rows = [slice(p * PAIR, (p + 1) * PAIR) for p in pairs]
    kb = [k[r].astype(bf16) for r in rows]
    kk = [_dot_nt(kb[p], kb[p]) for p in pairs]
    qk = [_dot_nt(q[rows[p]].astype(bf16), kb[p]) for p in pairs]

    chains = [(p, j) for p in pairs for j in range(2)]
    lane = lax.broadcasted_iota(jnp.int32, (1, LANES), 1)

    def column(r, idx):
        return jnp.sum(jnp.where(lane == idx, gcol_ref[0, r, :], 0.0), axis=-1, keepdims=True)

    beta = [column(rows[p], 2 * khead + j) for p, j in chains]
    gc = [column(rows[p], vh + 2 * khead + j) for p, j in chains]
    decay = [jnp.where(causal, jnp.exp(gc[i] - grow_ref[0, 2 + j:3 + j, rows[p]]), 0.0)
             for i, (p, j) in enumerate(chains)]
    X = []
    for g0 in range(0, len(chains), GROUP):
        grp = range(g0, min(g0 + GROUP, len(chains)))
        N = [jnp.where(strict, -(beta[i] * kk[chains[i][0]] * decay[i]), 0.0) for i in grp]
        Xg = [eye + n for n in N]
        Nb = [n.astype(bf16) for n in N]
        P = [_dot(nb, nb) for nb in Nb]
        for _ in range(levels):
            Pb = [m.astype(bf16) for m in P]
            PX = [_dot(Pb[i], jnp.concatenate([Pb[i], Xg[i].astype(bf16)], axis=1)) for i in range(len(grp))]
            P = [m[:, :PAIR] for m in PX]
            Xg = [Xg[i] + PX[i][:, PAIR:] for i in range(len(grp))]
        X += [Xg[i] + _dot(P[i].astype(bf16), Xg[i].astype(bf16)) for i in range(len(grp))]

    for i, (p, j) in enumerate(chains):
        qp, kp = q[rows[p]], k[rows[p]]
        eg = jnp.exp(gc[i])
        vp = v[rows[p], j * LANES:(j + 1) * LANES]
        rhs = jnp.concatenate([vp * beta[i], kp * (beta[i] * eg)], axis=1).astype(bf16)
        sol = _dot(X[i].astype(bf16), rhs)
        g_last = jnp.where(first, gc[i][CHUNK - 1:CHUNK, :], gc[i][PAIR - 1:PAIR, :])
        u_ref[0, j, rows[p], :] = sol[:, :LANES]
        w_ref[0, j, rows[p], :] = sol[:, LANES:].astype(bf16)
        qd_ref[0, j, rows[p], :] = (qp * eg).astype(bf16)
        kd_ref[0, j, rows[p], :] = (kp * jnp.exp(g_last - gc[i])).astype(bf16)
        aqk_ref[0, j, rows[p], :] = (qk[p] * decay[i]).astype(bf16)
        for c, r in ((2 * p, CHUNK - 1), (2 * p + 1, PAIR - 1)):
            egl_ref[0, j, c:c + 1, :] = jnp.broadcast_to(jnp.exp(gc[i][r:r + 1, :]), (1, LANES))


def _gdn_prep(P, cw, gcol, grow, l, offs):
    B, T, _ = P.shape
    off_q, off_k, off_v, off_z, Q, V = offs
    KH = Q // LANES
    VH = 2 * KH
    tt = _tile(T, 1024)
    hw = 2 * LANES
    cq, ck, cv = off_q // LANES, off_k // LANES, off_v // hw
    wk0, wv0 = Q // LANES, (2 * Q) // hw
    hb = tt // SUBLANES
    prev = lambda t: jnp.maximum(t * hb - 1, 0)
    head_out = lambda dt: (pl.BlockSpec((1, 2, tt, LANES), lambda b, h, t: (b, h, t, 0)),
                           jax.ShapeDtypeStruct((B, VH, T, LANES), dt))
    outs = [head_out(f32)] + [head_out(bf16) for _ in range(4)]
    outs.append((pl.BlockSpec((1, 2, tt // CHUNK, LANES), lambda b, h, t: (b, h, t, 0)),
                 jax.ShapeDtypeStruct((B, VH, T // CHUNK, LANES), f32)))
    return pl.pallas_call(
        functools.partial(_gdn_prep_kernel, tt=tt, vh=VH),
        grid=(B, KH, T // tt),
        in_specs=[
            pl.BlockSpec((1, tt, LANES), lambda b, h, t: (b, t, cq + h)),
            pl.BlockSpec((1, tt, LANES), lambda b, h, t: (b, t, ck + h)),
            pl.BlockSpec((1, tt, hw), lambda b, h, t: (b, t, cv + h)),
            pl.BlockSpec((1, SUBLANES, LANES), lambda b, h, t: (b, prev(t), cq + h)),
            pl.BlockSpec((1, SUBLANES, LANES), lambda b, h, t: (b, prev(t), ck + h)),
            pl.BlockSpec((1, SUBLANES, hw), lambda b, h, t: (b, prev(t), cv + h)),
            pl.BlockSpec((None, CONV_W, LANES), lambda b, h, t: (l, 0, h)),
            pl.BlockSpec((None, CONV_W, LANES), lambda b, h, t: (l, 0, wk0 + h)),
            pl.BlockSpec((None, CONV_W, hw), lambda b, h, t: (l, 0, wv0 + h)),
            pl.BlockSpec((1, tt, LANES), lambda b, h, t: (b, t, 0)),
            pl.BlockSpec((1, SUBLANES, tt), lambda b, h, t: (b, h, t)),
        ],
        out_specs=[o[0] for o in outs],
        out_shape=[o[1] for o in outs],
        scratch_shapes=[
            pltpu.VMEM((tt + SUBLANES, LANES), f32),
            pltpu.VMEM((tt + SUBLANES, LANES), f32),
            pltpu.VMEM((tt + SUBLANES, hw), f32),
        ],
        compiler_params=_cparams("parallel", "parallel", "parallel"),
        name="gdn_prep",
    )(P, P, P, P, P, P, cw, cw, cw, gcol, grow)


def _gdn_scan_kernel(u_ref, w_ref, qd_ref, kd_ref, aqk_ref, egl_ref, z_ref, nrm_ref, o_ref, s_ref, *, tt, nh):
    @pl.when(pl.program_id(2) == 0)
    def _():
        s_ref[...] = jnp.zeros_like(s_ref)

    nrm = nrm_ref[...]
    zeros = jnp.zeros((CHUNK, LANES), bf16)
    heads = range(nh)
    S = [s_ref[g] for g in heads]
    v_prev = [zeros for _ in heads]
    for c in range(tt // CHUNK):
        rows = slice(c * CHUNK, (c + 1) * CHUNK)
        ws_qs = [_dot(jnp.concatenate([w_ref[0, g, rows, :], qd_ref[0, g, rows, :]], axis=0),
                      S[g].astype(bf16)) for g in heads]
        v_new = [(u_ref[0, g, rows, :] - ws_qs[g][:CHUNK]).astype(bf16) for g in heads]
        v_pair = [jnp.concatenate([v_new[g], zeros] if c % 2 == 0 else [v_prev[g], v_new[g]], axis=0)
                  for g in heads]
        o = [ws_qs[g][CHUNK:] + _dot(aqk_ref[0, g, rows, :], v_pair[g]) for g in heads]
        S = [S[g] * egl_ref[0, g, c:c + 1, :] + _dot_tn(kd_ref[0, g, rows, :], v_new[g]) for g in heads]
        for g in heads:
            z = z_ref[0, rows, g * LANES:(g + 1) * LANES]
            o_ref[0, rows, g * LANES:(g + 1) * LANES] = (_rms(o[g]) * nrm * (z * _sigmoid(z))).astype(bf16)
        v_prev = v_new
    for g in heads:
        s_ref[g] = S[g]


def _gdn_scan(factors, P, nrm, l, off_z):
    u = factors[0]
    B, VH, T, _ = u.shape
    nh = _tile(VH, 8)
    tt = _tile(T, 512)
    cz = off_z // (nh * LANES)
    assert off_z % (nh * LANES) == 0 and nh % 2 == 0
    head_in = lambda: pl.BlockSpec((1, nh, tt, LANES), lambda b, h, t: (b, h, t, 0))
    return pl.pallas_call(
        functools.partial(_gdn_scan_kernel, tt=tt, nh=nh),
        grid=(B, VH // nh, T // tt),
        in_specs=[
            head_in(), head_in(), head_in(), head_in(), head_in(),
            pl.BlockSpec((1, nh, tt // CHUNK, LANES), lambda b, h, t: (b, h, t, 0)),
            pl.BlockSpec((1, tt, nh * LANES), lambda b, h, t: (b, t, cz + h)),
            pl.BlockSpec((None, 1, LANES), lambda b, h, t: (l, 0, 0)),
        ],
        out_specs=pl.BlockSpec((1, tt, nh * LANES), lambda b, h, t: (b, t, h)),
        out_shape=jax.ShapeDtypeStruct((B, T, VH * LANES), bf16),
        scratch_shapes=[pltpu.VMEM((nh, LANES, LANES), f32)],
        compiler_params=_cparams("parallel", "parallel", "arbitrary"),
        name="gdn_scan",
    )(*factors, P, nrm)


def _merge_kernel(yl_ref, yd_ref, gl_ref, gd_ref, wl_ref, wd_ref, o_ref):
    a = _dot(yl_ref[...], wl_ref[...])
    b = _dot(yd_ref[...], wd_ref[...])
    o_ref[...] = (_sigmoid(gl_ref[...]) * a + _sigmoid(gd_ref[...]) * b).astype(bf16)


def _out_proj_kernel(x_ref, m_ref, wo_ref, o_ref):
    o_ref[...] = x_ref[...] + _dot(m_ref[...], wo_ref[...])


def _merge(x, yl, yd, G, wl, wd, wo, l):
    M, D = x.shape
    R, V = yl.shape[1], yd.shape[1]
    tm, tn = _tile(M, 1024), _tile(D, 512)
    nd = D // tn
    merged = pl.pallas_call(
        _merge_kernel,
        grid=(M // tm, nd),
        in_specs=[
            pl.BlockSpec((tm, R), lambda i, j: (i, 0)),
            pl.BlockSpec((tm, V), lambda i, j: (i, 0)),
            pl.BlockSpec((tm, tn), lambda i, j: (i, j)),
            pl.BlockSpec((tm, tn), lambda i, j: (i, nd + j)),
            pl.BlockSpec((None, R, tn), lambda i, j: (l, 0, j)),
            pl.BlockSpec((None, V, tn), lambda i, j: (l, 0, j)),
        ],
        out_specs=pl.BlockSpec((tm, tn), lambda i, j: (i, j)),
        out_shape=jax.ShapeDtypeStruct((M, D), bf16),
        compiler_params=_cparams("parallel", "arbitrary"),
        name="merge",
    )(yl, yd, G, G, wl, wd)
    to = _tile(M, 512)
    return pl.pallas_call(
        _out_proj_kernel,
        grid=(M // to,),
        in_specs=[
            pl.BlockSpec((to, D), lambda i: (i, 0)),
            pl.BlockSpec((to, D), lambda i: (i, 0)),
            pl.BlockSpec((None, D, D), lambda i: (l, 0, 0)),
        ],
        out_specs=pl.BlockSpec((to, D), lambda i: (i, 0)),
        out_shape=jax.ShapeDtypeStruct((M, D), f32),
        compiler_params=_cparams("parallel"),
        name="out_proj",
    )(x, merged, wo)


def kernel(x, ffn1_norm, ffn1_w_in, ffn1_w_out, mix_norm, w_in, lru_conv_w, lru_conv_b, lru_gate_a_w, lru_gate_a_b, lru_gate_x_w, lru_gate_x_b, lru_lambda, dn_conv_w, dn_A_log, dn_dt_bias, dn_norm, w_branch_lru, w_branch_dn, w_out, ffn2_norm, ffn2_w_in, ffn2_w_out, final_norm):
    B, T, D = x.shape
    depth = ffn1_norm.shape[0]
    R = lru_conv_w.shape[-1]
    VH = dn_A_log.shape[-1]
    V = VH * LANES
    Q = (dn_conv_w.shape[-1] - V) // 2
    assert Q // LANES == VH // 2 and 2 * VH <= LANES and T % PAIR == 0
    off_q = 2 * R
    off_k = off_q + Q
    off_v = off_k + Q
    off_z = off_v + V
    off_b = off_z + V
    off_gates = off_b + 2 * VH
    assert w_in.shape[-1] == off_gates + 2 * D

    row = lambda p: p[:, None, :]
    w_main = w_in.astype(bf16)
    w_ba = jnp.pad(w_in[:, :, off_b:off_gates], ((0, 0), (0, 0), (0, LANES - 2 * VH))).astype(bf16)
    w_gates = w_in[:, :, off_gates:].astype(bf16)
    ffn1_wi, ffn1_wo = ffn1_w_in.astype(bf16), ffn1_w_out.astype(bf16)
    ffn2_wi, ffn2_wo = ffn2_w_in.astype(bf16), ffn2_w_out.astype(bf16)
    wl, wd, wo = w_branch_lru.astype(bf16), w_branch_dn.astype(bf16), w_out.astype(bf16)
    wa, wx = lru_gate_a_w.astype(bf16), lru_gate_x_w.astype(bf16)
    pad_a = lambda p: jnp.pad(p, ((0, 0), (VH, LANES - 2 * VH)))[:, None, :]
    alog, dtb = pad_a(dn_A_log), pad_a(dn_dt_bias)

    xm = x.reshape(B * T, D)
    for l in range(depth):
        xm = _ffn(xm, row(ffn1_norm), ffn1_wi, ffn1_wo, l, 0.5)
        P = _norm_proj(xm, row(mix_norm), w_main, l, "in_proj", n_cols=off_b).reshape(B, T, -1)
        ba = _norm_proj(xm, row(mix_norm), w_ba, l, "ba_proj").reshape(B, T, LANES)
        gates = _norm_proj(xm, row(mix_norm), w_gates, l, "gate_proj")
        y_lru = _lru(P, lru_conv_w, row(lru_conv_b), wa, row(lru_gate_a_b), wx, row(lru_gate_x_b),
                     row(lru_lambda), l, R)
        gcol, grow = _dn_gates(ba, alog, dtb, l, VH)
        factors = _gdn_prep(P, dn_conv_w, gcol, grow, l, (off_q, off_k, off_v, off_z, Q, V))
        y_dn = _gdn_scan(factors, P, row(dn_norm), l, off_z)
        xm = _merge(xm, y_lru.reshape(B * T, R), y_dn.reshape(B * T, V), gates, wl, wd, wo, l)
        xm = _ffn(xm, row(ffn2_norm), ffn2_wi, ffn2_wo, l, 0.5,
                  final_g=final_norm[None, :] if l == depth - 1 else None)
    return xm.reshape(B, T, D)
```

```python
import functools
import math

import numpy as np
import jax
import jax.numpy as jnp
from jax import lax
from jax.experimental import pallas as pl
from jax.experimental.pallas import tpu as pltpu

NORM_EPS = 1e-6
CHUNK = 64
PAIR = 2 * CHUNK
GROUP = 32
CONV_W = 4
LRU_C = 8.0
LANES = 128
SUBLANES = 8
VMEM_LIMIT = 56 * 2**20

f32 = jnp.float32
bf16 = jnp.bfloat16


def _cparams(*sem):
    return pltpu.CompilerParams(dimension_semantics=sem, vmem_limit_bytes=VMEM_LIMIT)


def _tile(n, pref):
    t = min(n, pref)
    assert n % t == 0, (n, pref)
    return t


def _sigmoid(x):
    return 0.5 * jnp.tanh(0.5 * x) + 0.5


def _softplus(x):
    return jnp.maximum(x, 0.0) + jnp.log1p(jnp.exp(-jnp.abs(x)))


def _rms(x):
    return x * lax.rsqrt(jnp.mean(x * x, axis=-1, keepdims=True) + NORM_EPS)


def _dot(a, b):
    return jnp.dot(a, b, preferred_element_type=f32)


def _dot_nt(a, b):
    return lax.dot_general(a, b, (((1,), (1,)), ((), ())), preferred_element_type=f32)


def _dot_tn(a, b):
    return lax.dot_general(a, b, (((0,), (0,)), ((), ())), preferred_element_type=f32)


def _ffn_kernel(x_ref, g_ref, wg_ref, wu_ref, wo_ref, *rest, scale, final):
    fg_ref, o_ref, h_ref = rest if final else (None,) + rest

    @pl.when(pl.program_id(1) == 0)
    def _():
        x = x_ref[...]
        h_ref[...] = (_rms(x) * g_ref[...]).astype(bf16)
        o_ref[...] = x

    h = h_ref[...]
    gate = _dot(h, wg_ref[...])
    up = _dot(h, wu_ref[...])
    act = (gate * _sigmoid(gate) * up).astype(bf16)
    o_ref[...] += scale * _dot(act, wo_ref[...])

    if final:
        @pl.when(pl.program_id(1) == pl.num_programs(1) - 1)
        def _():
            o_ref[...] = _rms(o_ref[...]) * fg_ref[...]


def _ffn(x, g, w_in, w_out, l, scale, final_g=None):
    M, D = x.shape
    final = final_g is not None
    F = w_out.shape[1]
    tm, tf = _tile(M, 1024), _tile(F, 512)
    nf = F // tf
    in_specs = [
        pl.BlockSpec((tm, D), lambda i, j: (i, 0)),
        pl.BlockSpec((None, 1, D), lambda i, j: (l, 0, 0)),
        pl.BlockSpec((None, D, tf), lambda i, j: (l, 0, j)),
        pl.BlockSpec((None, D, tf), lambda i, j: (l, 0, j + nf)),
        pl.BlockSpec((None, tf, D), lambda i, j: (l, j, 0)),
    ]
    args = [x, g, w_in, w_in, w_out]
    if final:
        in_specs.append(pl.BlockSpec((1, D), lambda i, j: (0, 0)))
        args.append(final_g)
    return pl.pallas_call(
        functools.partial(_ffn_kernel, scale=scale, final=final),
        grid=(M // tm, nf),
        in_specs=in_specs,
        out_specs=pl.BlockSpec((tm, D), lambda i, j: (i, 0)),
        out_shape=jax.ShapeDtypeStruct((M, D), f32),
        scratch_shapes=[pltpu.VMEM((tm, D), bf16)],
        compiler_params=_cparams("parallel", "arbitrary"),
        name="ffn",
    )(*args)


def _norm_proj_kernel(x_ref, g_ref, w_ref, o_ref, h_ref):
    @pl.when(pl.program_id(1) == 0)
    def _():
        h_ref[...] = (_rms(x_ref[...]) * g_ref[...]).astype(bf16)

    o_ref[...] = _dot(h_ref[...], w_ref[...])


def _norm_proj(x, g, w, l, name, n_cols=None):
    M, D = x.shape
    N = w.shape[2] if n_cols is None else n_cols
    tm, tn = _tile(M, 1024), _tile(N, 1024)
    return pl.pallas_call(
        _norm_proj_kernel,
        grid=(M // tm, N // tn),
        in_specs=[
            pl.BlockSpec((tm, D), lambda i, j: (i, 0)),
            pl.BlockSpec((None, 1, D), lambda i, j: (l, 0, 0)),
            pl.BlockSpec((None, D, tn), lambda i, j: (l, 0, j)),
        ],
        out_specs=pl.BlockSpec((tm, tn), lambda i, j: (i, j)),
        out_shape=jax.ShapeDtypeStruct((M, N), f32),
        scratch_shapes=[pltpu.VMEM((tm, D), bf16)],
        compiler_params=_cparams("parallel", "arbitrary"),
        name=name,
    )(x, g, w)


def _lru_kernel(xr_ref, yr_ref, cw_ref, cb_ref, wa_ref, ba_ref, wx_ref, bx_ref, lam_ref,
                o_ref, xe_ref, hc_ref, a_ref, u_ref, *, tt, tc):
    t = pl.program_id(2)

    @pl.when(t == 0)
    def _():
        xe_ref[0:SUBLANES, :] = jnp.zeros((SUBLANES, tc), f32)
        hc_ref[...] = jnp.zeros_like(hc_ref)

    @pl.when(t > 0)
    def _():
        xe_ref[0:SUBLANES, :] = xe_ref[tt:tt + SUBLANES, :]

    x = xr_ref[0]
    xe_ref[SUBLANES:SUBLANES + tt, :] = x
    cw = cw_ref[...]
    xc = cw[3:4] * x + cb_ref[...]
    for s in range(1, CONV_W):
        xc = xc + cw[3 - s:4 - s] * xe_ref[SUBLANES - s:SUBLANES - s + tt, :]

    ra, ri = [], []
    for blk in range(tc // LANES):
        xb = xc[:, blk * LANES:(blk + 1) * LANES].astype(bf16)
        ra.append(_dot(xb, wa_ref[blk]))
        ri.append(_dot(xb, wx_ref[blk]))
    r = _sigmoid(jnp.concatenate(ra, axis=1) + ba_ref[...])
    i = _sigmoid(jnp.concatenate(ri, axis=1) + bx_ref[...])
    log_a = (-LRU_C) * r * _softplus(-lam_ref[...])
    a = jnp.exp(log_a)
    u = xc * i * jnp.exp(0.5 * jnp.log(-jnp.tanh(log_a) * (a * a + 1.0)))
    a_ref[...] = a
    u_ref[...] = u

    row = lax.broadcasted_iota(jnp.int32, (SUBLANES, tc), 0)

    def group(g, h):
        r0 = pl.multiple_of(g * SUBLANES, SUBLANES)
        A = a_ref[pl.ds(r0, SUBLANES), :]
        B = u_ref[pl.ds(r0, SUBLANES), :]
        for s in (1, 2, 4):
            keep = row >= s
            B = jnp.where(keep, A * pltpu.roll(B, s, 0) + B, B)
            A = jnp.where(keep, A * pltpu.roll(A, s, 0), A)
        H = A * h + B
        u_ref[pl.ds(r0, SUBLANES), :] = H
        return H[SUBLANES - 1:SUBLANES, :]

    hc_ref[0:1, :] = lax.fori_loop(0, tt // SUBLANES, group, hc_ref[0:1, :], unroll=4)

    y = yr_ref[0]
    gelu = 0.5 * y * (1.0 + jnp.tanh(math.sqrt(2.0 / math.pi) * (y + 0.044715 * (y * y * y))))
    o_ref[0] = (u_ref[...] * gelu).astype(bf16)


def _lru(P, cw, cb, wa, ba, wx, bx, lam, l, R):
    B, T, _ = P.shape
    tt, tc = _tile(T, 512), _tile(R, 512)
    nc = R // tc
    nb = tc // LANES
    vec = lambda: pl.BlockSpec((None, 1, tc), lambda b, c, t: (l, 0, c))
    gw = lambda: pl.BlockSpec((None, nb, LANES, LANES), lambda b, c, t: (l, c, 0, 0))
    return pl.pallas_call(
        functools.partial(_lru_kernel, tt=tt, tc=tc),
        grid=(B, nc, T // tt),
        in_specs=[
            pl.BlockSpec((1, tt, tc), lambda b, c, t: (b, t, c)),
            pl.BlockSpec((1, tt, tc), lambda b, c, t: (b, t, c + nc)),
            pl.BlockSpec((None, CONV_W, tc), lambda b, c, t: (l, 0, c)),
            vec(), gw(), vec(), gw(), vec(), vec(),
        ],
        out_specs=pl.BlockSpec((1, tt, tc), lambda b, c, t: (b, t, c)),
        out_shape=jax.ShapeDtypeStruct((B, T, R), bf16),
        scratch_shapes=[
            pltpu.VMEM((tt + SUBLANES, tc), f32),
            pltpu.VMEM((SUBLANES, tc), f32),
            pltpu.VMEM((tt, tc), f32),
            pltpu.VMEM((tt, tc), f32),
        ],
        compiler_params=_cparams("parallel", "parallel", "arbitrary"),
        name="lru",
    )(P, P, cw, cb, wa, ba, wx, bx, lam)


def _dn_gate_kernel(ba_ref, alog_ref, dtb_ref, tril_ref, prow_ref, gcol_ref, grow_ref, *, vh):
    ba = ba_ref[0]
    lane = lax.broadcasted_iota(jnp.int32, ba.shape, 1)
    beta = _sigmoid(ba)
    g = -jnp.exp(alog_ref[...]) * _softplus(ba + dtb_ref[...])
    gcum = jnp.dot(tril_ref[...], g, preferred_element_type=f32, precision=lax.Precision.HIGHEST)
    vals = jnp.where(lane < vh, beta, gcum)
    gcol_ref[0] = vals
    grow_ref[0] = lax.dot_general(prow_ref[...], vals, (((1,), (1,)), ((), ())),
                                  preferred_element_type=f32, precision=lax.Precision.HIGHEST)


def _dn_gate_consts(vh, tt):
    kh = vh // 2
    prow = np.zeros((kh * SUBLANES, LANES), np.float32)
    for p in range(kh):
        for m, src in enumerate((2 * p, 2 * p + 1, vh + 2 * p, vh + 2 * p + 1)):
            prow[p * SUBLANES + m, src] = 1.0
    idx = np.arange(tt)
    tril = ((idx[:, None] // CHUNK == idx[None, :] // CHUNK) & (idx[None, :] <= idx[:, None])).astype(np.float32)
    return jnp.asarray(tril), jnp.asarray(prow)


def _dn_gates(ba, alog, dtb, l, vh):
    B, T, _ = ba.shape
    kh = vh // 2
    tt = _tile(T, 256)
    tril, prow = _dn_gate_consts(vh, tt)
    full = lambda a: pl.BlockSpec(a.shape, lambda b, t: (0,) * a.ndim)
    return pl.pallas_call(
        functools.partial(_dn_gate_kernel, vh=vh),
        grid=(B, T // tt),
        in_specs=[
            pl.BlockSpec((1, tt, LANES), lambda b, t: (b, t, 0)),
            pl.BlockSpec((None, 1, LANES), lambda b, t: (l, 0, 0)),
            pl.BlockSpec((None, 1, LANES), lambda b, t: (l, 0, 0)),
            full(tril), full(prow),
        ],
        out_specs=[
            pl.BlockSpec((1, tt, LANES), lambda b, t: (b, t, 0)),
            pl.BlockSpec((1, kh * SUBLANES, tt), lambda b, t: (b, 0, t)),
        ],
        out_shape=[
            jax.ShapeDtypeStruct((B, T, LANES), f32),
            jax.ShapeDtypeStruct((B, kh * SUBLANES, T), f32),
        ],
        compiler_params=_cparams("parallel", "parallel"),
        name="dn_gates",
    )(ba, alog, dtb, tril, prow)


def _conv_silu(x_ref, halo_ref, w_ref, xe_ref, t, tt):
    xe_ref[0:SUBLANES, :] = jnp.where(t > 0, halo_ref[0], 0.0)
    x = x_ref[0]
    xe_ref[SUBLANES:SUBLANES + tt, :] = x
    w = w_ref[...]
    y = w[3:4] * x
    for s in range(1, CONV_W):
        y = y + w[3 - s:4 - s] * xe_ref[SUBLANES - s:SUBLANES - s + tt, :]
    return y * _sigmoid(y)


def _gdn_prep_kernel(q_ref, k_ref, v_ref, qh_ref, kh_ref, vh_ref, wq_ref, wk_ref, wv_ref, gcol_ref, grow_ref,
                     u_ref, w_ref, qd_ref, kd_ref, aqk_ref, egl_ref, qe_ref, ke_ref, ve_ref, *, tt, vh):
    t = pl.program_id(2)
    khead = pl.program_id(1)
    q = _conv_silu(q_ref, qh_ref, wq_ref, qe_ref, t, tt)
    k = _conv_silu(k_ref, kh_ref, wk_ref, ke_ref, t, tt)
    v = _conv_silu(v_ref, vh_ref, wv_ref, ve_ref, t, tt)
    q = q * lax.rsqrt(jnp.sum(q * q, axis=-1, keepdims=True) + NORM_EPS) * (LANES ** -0.5)
    k = k * lax.rsqrt(jnp.sum(k * k, axis=-1, keepdims=True) + NORM_EPS)

    ri = lax.broadcasted_iota(jnp.int32, (PAIR, PAIR), 0)
    ci = lax.broadcasted_iota(jnp.int32, (PAIR, PAIR), 1)
    same_chunk = (ri >= CHUNK) == (ci >= CHUNK)
    causal = same_chunk & (ci <= ri)
    strict = same_chunk & (ci < ri)
    eye = (ri == ci).astype(f32)
    first = lax.broadcasted_iota(jnp.int32, (PAIR, 1), 0) < CHUNK
    levels = int(math.log2(CHUNK)) - 2

    pairs = range(tt // PAIR)
    rows = [slice(p * PAIR, (p + 1) * PAIR) for p in pairs]
    kb = [k[r].astype(bf16) for r in rows]
    kk = [_dot_nt(kb[p], kb[p]) for p in pairs]
    qk = [_dot_nt(q[rows[p]].astype(bf16), kb[p]) for p in pairs]

    chains = [(p, j) for p in pairs for j in range(2)]
    lane = lax.broadcasted_iota(jnp.int32, (1, LANES), 1)

    def column(r, idx):
        return jnp.sum(jnp.where(lane == idx, gcol_ref[0, r, :], 0.0), axis=-1, keepdims=True)

    beta = [column(rows[p], 2 * khead + j) for p, j in chains]
    gc = [column(rows[p], vh + 2 * khead + j) for p, j in chains]
    decay = [jnp.where(causal, jnp.exp(gc[i] - grow_ref[0, 2 + j:3 + j, rows[p]]), 0.0)
             for i, (p, j) in enumerate(chains)]
    X = []
    for g0 in range(0, len(chains), GROUP):
        grp = range(g0, min(g0 + GROUP, len(chains)))
        N = [jnp.where(strict, -(beta[i] * kk[chains[i][0]] * decay[i]), 0.0) for i in grp]
        Xg = [eye + n for n in N]
        Nb = [n.astype(bf16) for n in N]
        P = [_dot(nb, nb) for nb in Nb]
        for _ in range(levels):
            Pb = [m.astype(bf16) for m in P]
            PX = [_dot(Pb[i], jnp.concatenate([Pb[i], Xg[i].astype(bf16)], axis=1)) for i in range(len(grp))]
            P = [m[:, :PAIR] for m in PX]
            Xg = [Xg[i] + PX[i][:, PAIR:] for i in range(len(grp))]
        X += [Xg[i] + _dot(P[i].astype(bf16), Xg[i].astype(bf16)) for i in range(len(grp))]

    for i, (p, j) in enumerate(chains):
        qp, kp = q[rows[p]], k[rows[p]]
        eg = jnp.exp(gc[i])
        vp = v[rows[p], j * LANES:(j + 1) * LANES]
        rhs = jnp.concatenate([vp * beta[i], kp * (beta[i] * eg)], axis=1).astype(bf16)
        sol = _dot(X[i].astype(bf16), rhs)
        g_last = jnp.where(first, gc[i][CHUNK - 1:CHUNK, :], gc[i][PAIR - 1:PAIR, :])
        u_ref[0, j, rows[p], :] = sol[:, :LANES]
        w_ref[0, j, rows[p], :] = sol[:, LANES:].astype(bf16)
        qd_ref[0, j, rows[p], :] = (qp * eg).astype(bf16)
        kd_ref[0, j, rows[p], :] = (kp * jnp.exp(g_last - gc[i])).astype(bf16)
        aqk_ref[0, j, rows[p], :] = (qk[p] * decay[i]).astype(bf16)
        for c, r in ((2 * p, CHUNK - 1), (2 * p + 1, PAIR - 1)):
            egl_ref[0, j, c:c + 1, :] = jnp.broadcast_to(jnp.exp(gc[i][r:r + 1, :]), (1, LANES))


def _gdn_prep(P, cw, gcol, grow, l, offs):
    B, T, _ = P.shape
    off_q, off_k, off_v, off_z, Q, V = offs
    KH = Q // LANES
    VH = 2 * KH
    tt = _tile(T, 2048)
    hw = 2 * LANES
    cq, ck, cv = off_q // LANES, off_k // LANES, off_v // hw
    wk0, wv0 = Q // LANES, (2 * Q) // hw
    hb = tt // SUBLANES
    prev = lambda t: jnp.maximum(t * hb - 1, 0)
    head_out = lambda dt: (pl.BlockSpec((1, 2, tt, LANES), lambda b, h, t: (b, h, t, 0)),
                           jax.ShapeDtypeStruct((B, VH, T, LANES), dt))
    outs = [head_out(f32)] + [head_out(bf16) for _ in range(4)]
    outs.append((pl.BlockSpec((1, 2, tt // CHUNK, LANES), lambda b, h, t: (b, h, t, 0)),
                 jax.ShapeDtypeStruct((B, VH, T // CHUNK, LANES), f32)))
    return pl.pallas_call(
        functools.partial(_gdn_prep_kernel, tt=tt, vh=VH),
        grid=(B, KH, T // tt),
        in_specs=[
            pl.BlockSpec((1, tt, LANES), lambda b, h, t: (b, t, cq + h)),
            pl.BlockSpec((1, tt, LANES), lambda b, h, t: (b, t, ck + h)),
            pl.BlockSpec((1, tt, hw), lambda b, h, t: (b, t, cv + h)),
            pl.BlockSpec((1, SUBLANES, LANES), lambda b, h, t: (b, prev(t), cq + h)),
            pl.BlockSpec((1, SUBLANES, LANES), lambda b, h, t: (b, prev(t), ck + h)),
            pl.BlockSpec((1, SUBLANES, hw), lambda b, h, t: (b, prev(t), cv + h)),
            pl.BlockSpec((None, CONV_W, LANES), lambda b, h, t: (l, 0, h)),
            pl.BlockSpec((None, CONV_W, LANES), lambda b, h, t: (l, 0, wk0 + h)),
            pl.BlockSpec((None, CONV_W, hw), lambda b, h, t: (l, 0, wv0 + h)),
            pl.BlockSpec((1, tt, LANES), lambda b, h, t: (b, t, 0)),
            pl.BlockSpec((1, SUBLANES, tt), lambda b, h, t: (b, h, t)),
        ],
        out_specs=[o[0] for o in outs],
        out_shape=[o[1] for o in outs],
        scratch_shapes=[
            pltpu.VMEM((tt + SUBLANES, LANES), f32),
            pltpu.VMEM((tt + SUBLANES, LANES), f32),
            pltpu.VMEM((tt + SUBLANES, hw), f32),
        ],
        compiler_params=_cparams("parallel", "parallel", "parallel"),
        name="gdn_prep",
    )(P, P, P, P, P, P, cw, cw, cw, gcol, grow)


def _gdn_scan_kernel(u_ref, w_ref, qd_ref, kd_ref, aqk_ref, egl_ref, z_ref, nrm_ref, o_ref, s_ref, *, tt, nh):
    @pl.when(pl.program_id(2) == 0)
    def _():
        s_ref[...] = jnp.zeros_like(s_ref)

    nrm = nrm_ref[...]
    zeros = jnp.zeros((CHUNK, LANES), bf16)
    heads = range(nh)
    S = [s_ref[g] for g in heads]
    v_prev = [zeros for _ in heads]
    for c in range(tt // CHUNK):
        rows = slice(c * CHUNK, (c + 1) * CHUNK)
        ws_qs = [_dot(jnp.concatenate([w_ref[0, g, rows, :], qd_ref[0, g, rows, :]], axis=0),
                      S[g].astype(bf16)) for g in heads]
        v_new = [(u_ref[0, g, rows, :] - ws_qs[g][:CHUNK]).astype(bf16) for g in heads]
        v_pair = [jnp.concatenate([v_new[g], zeros] if c % 2 == 0 else [v_prev[g], v_new[g]], axis=0)
                  for g in heads]
        o = [ws_qs[g][CHUNK:] + _dot(aqk_ref[0, g, rows, :], v_pair[g]) for g in heads]
        S = [S[g] * egl_ref[0, g, c:c + 1, :] + _dot_tn(kd_ref[0, g, rows, :], v_new[g]) for g in heads]
        for g in heads:
            z = z_ref[0, rows, g * LANES:(g + 1) * LANES]
            o_ref[0, rows, g * LANES:(g + 1) * LANES] = (_rms(o[g]) * nrm * (z * _sigmoid(z))).astype(bf16)
        v_prev = v_new
    for g in heads:
        s_ref[g] = S[g]


def _gdn_scan(factors, P, nrm, l, off_z):
    u = factors[0]
    B, VH, T, _ = u.shape
    nh = _tile(VH, 8)
    tt = _tile(T, 512)
    cz = off_z // (nh * LANES)
    assert off_z % (nh * LANES) == 0 and nh % 2 == 0
    head_in = lambda: pl.BlockSpec((1, nh, tt, LANES), lambda b, h, t: (b, h, t, 0))
    return pl.pallas_call(
        functools.partial(_gdn_scan_kernel, tt=tt, nh=nh),
        grid=(B, VH // nh, T // tt),
        in_specs=[
            head_in(), head_in(), head_in(), head_in(), head_in(),
            pl.BlockSpec((1, nh, tt // CHUNK, LANES), lambda b, h, t: (b, h, t, 0)),
            pl.BlockSpec((1, tt, nh * LANES), lambda b, h, t: (b, t, cz + h)),
            pl.BlockSpec((None, 1, LANES), lambda b, h, t: (l, 0, 0)),
        ],
        out_specs=pl.BlockSpec((1, tt, nh * LANES), lambda b, h, t: (b, t, h)),
        out_shape=jax.ShapeDtypeStruct((B, T, VH * LANES), bf16),
        scratch_shapes=[pltpu.VMEM((nh, LANES, LANES), f32)],
        compiler_params=_cparams("parallel", "parallel", "arbitrary"),
        name="gdn_scan",
    )(*factors, P, nrm)


def _merge_kernel(yl_ref, yd_ref, gl_ref, gd_ref, wl_ref, wd_ref, o_ref):
    a = _dot(yl_ref[...], wl_ref[...])
    b = _dot(yd_ref[...], wd_ref[...])
    o_ref[...] = (_sigmoid(gl_ref[...]) * a + _sigmoid(gd_ref[...]) * b).astype(bf16)


def _out_proj_kernel(x_ref, m_ref, wo_ref, o_ref):
    o_ref[...] = x_ref[...] + _dot(m_ref[...], wo_ref[...])


def _merge(x, yl, yd, G, wl, wd, wo, l):
    M, D = x.shape
    R, V = yl.shape[1], yd.shape[1]
    tm, tn = _tile(M, 1024), _tile(D, 512)
    nd = D // tn
    merged = pl.pallas_call(
        _merge_kernel,
        grid=(M // tm, nd),
        in_specs=[
            pl.BlockSpec((tm, R), lambda i, j: (i, 0)),
            pl.BlockSpec((tm, V), lambda i, j: (i, 0)),
            pl.BlockSpec((tm, tn), lambda i, j: (i, j)),
            pl.BlockSpec((tm, tn), lambda i, j: (i, nd + j)),
            pl.BlockSpec((None, R, tn), lambda i, j: (l, 0, j)),
            pl.BlockSpec((None, V, tn), lambda i, j: (l, 0, j)),
        ],
        out_specs=pl.BlockSpec((tm, tn), lambda i, j: (i, j)),
        out_shape=jax.ShapeDtypeStruct((M, D), bf16),
        compiler_params=_cparams("parallel", "arbitrary"),
        name="merge",
    )(yl, yd, G, G, wl, wd)
    to = _tile(M, 512)
    return pl.pallas_call(
        _out_proj_kernel,
        grid=(M // to,),
        in_specs=[
            pl.BlockSpec((to, D), lambda i: (i, 0)),
            pl.BlockSpec((to, D), lambda i: (i, 0)),
            pl.BlockSpec((None, D, D), lambda i: (l, 0, 0)),
        ],
        out_specs=pl.BlockSpec((to, D), lambda i: (i, 0)),
        out_shape=jax.ShapeDtypeStruct((M, D), f32),
        compiler_params=_cparams("parallel"),
        name="out_proj",
    )(x, merged, wo)


def kernel(x, ffn1_norm, ffn1_w_in, ffn1_w_out, mix_norm, w_in, lru_conv_w, lru_conv_b, lru_gate_a_w, lru_gate_a_b, lru_gate_x_w, lru_gate_x_b, lru_lambda, dn_conv_w, dn_A_log, dn_dt_bias, dn_norm, w_branch_lru, w_branch_dn, w_out, ffn2_norm, ffn2_w_in, ffn2_w_out, final_norm):
    B, T, D = x.shape
    depth = ffn1_norm.shape[0]
    R = lru_conv_w.shape[-1]
    VH = dn_A_log.shape[-1]
    V = VH * LANES
    Q = (dn_conv_w.shape[-1] - V) // 2
    assert Q // LANES == VH // 2 and 2 * VH <= LANES and T % PAIR == 0
    off_q = 2 * R
    off_k = off_q + Q
    off_v = off_k + Q
    off_z = off_v + V
    off_b = off_z + V
    off_gates = off_b + 2 * VH
    assert w_in.shape[-1] == off_gates + 2 * D

    row = lambda p: p[:, None, :]
    w_main = w_in.astype(bf16)
    w_ba = jnp.pad(w_in[:, :, off_b:off_gates], ((0, 0), (0, 0), (0, LANES - 2 * VH))).astype(bf16)
    w_gates = w_in[:, :, off_gates:].astype(bf16)
    ffn1_wi, ffn1_wo = ffn1_w_in.astype(bf16), ffn1_w_out.astype(bf16)
    ffn2_wi, ffn2_wo = ffn2_w_in.astype(bf16), ffn2_w_out.astype(bf16)
    wl, wd, wo = w_branch_lru.astype(bf16), w_branch_dn.astype(bf16), w_out.astype(bf16)
    wa, wx = lru_gate_a_w.astype(bf16), lru_gate_x_w.astype(bf16)
    pad_a = lambda p: jnp.pad(p, ((0, 0), (VH, LANES - 2 * VH)))[:, None, :]
    alog, dtb = pad_a(dn_A_log), pad_a(dn_dt_bias)

    xm = x.reshape(B * T, D)
    for l in range(depth):
        xm = _ffn(xm, row(ffn1_norm), ffn1_wi, ffn1_wo, l, 0.5)
        P = _norm_proj(xm, row(mix_norm), w_main, l, "in_proj", n_cols=off_b).reshape(B, T, -1)
        ba = _norm_proj(xm, row(mix_norm), w_ba, l, "ba_proj").reshape(B, T, LANES)
        gates = _norm_proj(xm, row(mix_norm), w_gates, l, "gate_proj")
        y_lru = _lru(P, lru_conv_w, row(lru_conv_b), wa, row(lru_gate_a_b), wx, row(lru_gate_x_b),
                     row(lru_lambda), l, R)
        gcol, grow = _dn_gates(ba, alog, dtb, l, VH)
        factors = _gdn_prep(P, dn_conv_w, gcol, grow, l, (off_q, off_k, off_v, off_z, Q, V))
        y_dn = _gdn_scan(factors, P, row(dn_norm), l, off_z)
        xm = _merge(xm, y_lru.reshape(B * T, R), y_dn.reshape(B * T, V), gates, wl, wd, wo, l)
        xm = _ffn(xm, row(ffn2_norm), ffn2_wi, ffn2_wo, l, 0.5,
                  final_g=final_norm[None, :] if l == depth - 1 else None)
    return xm.reshape(B, T, D)
```
